```python
import math
import jax
import jax.numpy as jnp
from jax import lax
import numpy as np

D_MODEL = 1024
BATCH = 2
SEQ = 8192
DEPTH = 4
DEC_BATCH = 128
DEC_SEQ = 1
PAST_LEN = 8192
PAGE_SIZE = 128

N_MIXERS = 3
N_MOBA = (DEPTH + 2) // 3
N_NSA = (DEPTH + 1) // 3
N_MLA = DEPTH // 3

D_PLE = 256
EPS = 1e-6
NEG_INF = -1e30
Q_BLOCK = 128

HEAD_DIM = 64
N_HEADS = D_MODEL // HEAD_DIM
KV_HEADS = N_HEADS // 4
GROUP = N_HEADS // KV_HEADS
ATT_WIDTH = N_HEADS * HEAD_DIM
KV_WIDTH = KV_HEADS * HEAD_DIM
ATT_SCALE = HEAD_DIM ** -0.5

N_BUCKETS = 32
MAX_EXACT = N_BUCKETS // 2
MAX_DISTANCE = 2048

MOBA_BLOCK = 256
MOBA_TOPK = 3
MOBA_IN = 2 * ATT_WIDTH + 2 * KV_WIDTH

CMP_STRIDE = 16
CMP_LEN = 2 * CMP_STRIDE
CMP_HIDDEN = 2 * HEAD_DIM
SLC_BLOCK = 64
SLC_TOPN = 16
WINDOW = 512
NSA_SPLITS = tuple(ATT_WIDTH + KV_WIDTH * i for i in range(7)) + (ATT_WIDTH + 6 * KV_WIDTH + 3 * N_HEADS,)
NSA_IN = NSA_SPLITS[-1] + ATT_WIDTH

MLA_HEADS = N_HEADS
Q_LORA = 384
KV_LORA = 256
NOPE_DIM = 64
ROPE_DIM = 32
V_DIM = 64
ROPE_THETA = 10000.0
MLA_SCALE = (NOPE_DIM + ROPE_DIM) ** -0.5
MLA_SPLITS = (Q_LORA, Q_LORA + KV_LORA, Q_LORA + KV_LORA + ROPE_DIM)
MLA_IN = MLA_SPLITS[-1] + MLA_HEADS * V_DIM

kernel_name = 'hybrid_moba_nsa_mla_step'


def rms_norm(x, g):
    xf = x.astype(jnp.float32)
    y = xf * lax.rsqrt(jnp.mean(xf * xf, axis=-1, keepdims=True) + EPS)
    return (y * g.astype(jnp.float32)).astype(x.dtype)


def masked_softmax(logits, mask):
    l = jnp.where(mask, logits.astype(jnp.float32), NEG_INF)
    e = jnp.where(mask, jnp.exp(l - jnp.max(l, axis=-1, keepdims=True)), 0.0)
    return e / jnp.maximum(jnp.sum(e, axis=-1, keepdims=True), 1e-30)


def t5_bucket(dist):
    n = jnp.maximum(dist, 0)
    nf = jnp.maximum(n, 1).astype(jnp.float32)
    big = MAX_EXACT + (jnp.log(nf / MAX_EXACT) / math.log(MAX_DISTANCE / MAX_EXACT)
                       * (N_BUCKETS - MAX_EXACT)).astype(jnp.int32)
    return jnp.where(n < MAX_EXACT, n, jnp.minimum(big, N_BUCKETS - 1))


def rope(x, pos):
    half = x.shape[-1] // 2
    freq = ROPE_THETA ** (-jnp.arange(half, dtype=jnp.float32) / half)
    ang = pos[..., None].astype(jnp.float32) * freq
    cos, sin = jnp.cos(ang), jnp.sin(ang)
    x1, x2 = x[..., :half].astype(jnp.float32), x[..., half:].astype(jnp.float32)
    return jnp.concatenate([x1 * cos - x2 * sin, x1 * sin + x2 * cos], axis=-1).astype(x.dtype)


def to_blocks(a, size):
    length = a.shape[-3]
    nb = -(-length // size)
    pad = [(0, 0)] * a.ndim
    pad[-3] = (0, nb * size - length)
    return jnp.pad(a, pad).reshape(*a.shape[:-3], nb, size, *a.shape[-2:])


def sweep_query_blocks(fn, batch, seq):
    nq = seq // Q_BLOCK
    def one(i):
        return fn(i // nq, (i % nq) * Q_BLOCK)
    out = lax.map(one, jnp.arange(batch * nq, dtype=jnp.int32))
    return out.reshape(batch, seq, *out.shape[2:])


def q_block(a, bi, start):
    return lax.dynamic_slice_in_dim(a[bi], start, Q_BLOCK, axis=0)


def gather_past(pool, j, pt):
    rows = pool[j, pt]
    return rows.reshape(-1, *rows.shape[2:])


def gated_out(o, z, w_out):
    return (o.reshape(z.shape) * jax.nn.silu(z)) @ w_out


def add_ple(h, p, w_proj, w_gate, g_norm):
    return h + jax.nn.sigmoid(rms_norm(h, g_norm) @ w_gate) * (p @ w_proj)


def moba_project(h, w_in, q_gain, k_gain):
    q, k, v, z = jnp.split(h @ w_in, (ATT_WIDTH, ATT_WIDTH + KV_WIDTH, ATT_WIDTH + 2 * KV_WIDTH), axis=-1)
    lead = h.shape[:-1]
    q = rms_norm(q.reshape(*lead, N_HEADS, HEAD_DIM), q_gain)
    k = rms_norm(k.reshape(*lead, KV_HEADS, HEAD_DIM), k_gain)
    return q, k, v.reshape(*lead, KV_HEADS, HEAD_DIM), z


def moba_core(q, q_pos, k_blk, v_blk, table):
    t = q.shape[0]
    n_blk = k_blk.shape[0]
    qg = q.reshape(t, KV_HEADS, GROUP, HEAD_DIM)
    k_mean = jnp.mean(k_blk.astype(jnp.float32), axis=1)
    gate = jnp.einsum('tgrd,ngd->tgrn', qg.astype(jnp.float32), k_mean)
    own = q_pos // MOBA_BLOCK
    own4 = own[:, None, None, None]
    gate = jnp.where(jnp.arange(n_blk) < own4, gate, NEG_INF)
    _, top = lax.top_k(gate, min(MOBA_TOPK, n_blk))
    own_b = jnp.broadcast_to(own4, (t, KV_HEADS, GROUP, 1))
    idx = jnp.concatenate([top, own_b], axis=-1)
    blk_ok = jnp.concatenate([top < own4, jnp.ones(own_b.shape, bool)], axis=-1)
    g_idx = jnp.arange(KV_HEADS)[None, :, None, None]
    ks = k_blk.transpose(2, 0, 1, 3)[g_idx, idx]
    vs = v_blk.transpose(2, 0, 1, 3)[g_idx, idx]
    n_sel = idx.shape[-1]
    kpos = idx[..., None] * MOBA_BLOCK + jnp.arange(MOBA_BLOCK)
    dist = q_pos[:, None, None, None, None] - kpos
    head = jnp.arange(N_HEADS).reshape(KV_HEADS, GROUP)[None, :, :, None, None]
    logits = (jnp.einsum('tgrd,tgrsbd->tgrsb', qg, ks).astype(jnp.float32) * ATT_SCALE
              + table[t5_bucket(dist), head])
    mask = blk_ok[..., None] & (dist >= 0)
    p = masked_softmax(logits.reshape(t, KV_HEADS, GROUP, -1), mask.reshape(t, KV_HEADS, GROUP, -1))
    o = jnp.einsum('tgrk,tgrkd->tgrd', p.astype(vs.dtype),
                   vs.reshape(t, KV_HEADS, GROUP, n_sel * MOBA_BLOCK, HEAD_DIM))
    return o.reshape(t, N_HEADS, HEAD_DIM)


def moba_prompt(q, k, v, table):
    b, s = q.shape[:2]
    kb, vb = to_blocks(k, MOBA_BLOCK), to_blocks(v, MOBA_BLOCK)
    def block(bi, start):
        return moba_core(q_block(q, bi, start), start + jnp.arange(Q_BLOCK), kb[bi], vb[bi], table)
    return sweep_query_blocks(block, b, s)


def moba_sample(q, k, v, q_pos, pool_k, pool_v, j, page_table, table):
    def one(args):
        qb, k_new, v_new, pt = args
        kb = to_blocks(jnp.concatenate([gather_past(pool_k, j, pt), k_new], axis=0), MOBA_BLOCK)
        vb = to_blocks(jnp.concatenate([gather_past(pool_v, j, pt), v_new], axis=0), MOBA_BLOCK)
        return moba_core(qb, q_pos, kb, vb, table)
    return lax.map(one, (q, k, v, page_table))


def nsa_project(h, w_in, q_gain, ks_gain, kw_gain):
    q, kc, vc, ks, vs, kw, vw, g, z = jnp.split(h @ w_in, NSA_SPLITS, axis=-1)
    lead = h.shape[:-1]
    kvh = lambda a: a.reshape(*lead, KV_HEADS, HEAD_DIM)
    q = rms_norm(q.reshape(*lead, N_HEADS, HEAD_DIM), q_gain)
    gates = jax.nn.sigmoid(g).reshape(*lead, N_HEADS, 3)
    return q, gates, kvh(kc), kvh(vc), rms_norm(kvh(ks), ks_gain), kvh(vs), rms_norm(kvh(kw), kw_gain), kvh(vw), z


def nsa_compress(k, v, cmp):
    pos_emb, w1k, w2k, w1v, w2v, kc_gain = cmp
    length = k.shape[0]
    nc = length // CMP_STRIDE
    pad = (nc + 1) * CMP_STRIDE - length
    def windows(a):
        r = jnp.pad(a, ((0, pad), (0, 0), (0, 0))).reshape(nc + 1, CMP_STRIDE, KV_HEADS, HEAD_DIM)
        w = jnp.concatenate([r[:-1], r[1:]], axis=1) + pos_emb[None, :, None, :]
        return w.transpose(0, 2, 1, 3).reshape(nc, KV_HEADS, CMP_LEN * HEAD_DIM)
    k_cmp = jax.nn.gelu(windows(k) @ w1k) @ w2k
    v_cmp = jax.nn.gelu(windows(v) @ w1v) @ w2v
    return rms_norm(k_cmp, kc_gain), v_cmp


def nsa_core(q, q_pos, gates, k_cmp, v_cmp, k_blk, v_blk, k_win, v_win, win_pos, table):
    t = q.shape[0]
    qg = q.reshape(t, KV_HEADS, GROUP, HEAD_DIM)
    qp = q_pos[:, None, None, None]
    head = jnp.arange(N_HEADS).reshape(KV_HEADS, GROUP)[None, :, :, None]
    n_cmp = k_cmp.shape[0]
    c_start = jnp.arange(n_cmp) * CMP_STRIDE
    c_end = c_start + CMP_LEN - 1
    lc = jnp.einsum('tgrd,cgd->tgrc', qg, k_cmp).astype(jnp.float32) * ATT_SCALE
    pc = masked_softmax(lc + table[t5_bucket(qp - c_end), head], c_end <= qp)
    o_cmp = jnp.einsum('tgrc,cgd->tgrd', pc.astype(v_cmp.dtype), v_cmp)
    n_blk = k_blk.shape[0]
    b_start = jnp.arange(n_blk) * SLC_BLOCK
    overlap = ((c_start[:, None] < b_start[None, :] + SLC_BLOCK)
               & (c_end[:, None] >= b_start[None, :])).astype(jnp.float32)
    imp = jnp.einsum('tgrc,cn->tgn', pc, overlap)
    own = q_pos // SLC_BLOCK
    imp = jnp.where(jnp.arange(n_blk)[None, None, :] < own[:, None, None], imp, NEG_INF)
    _, top = lax.top_k(imp, min(SLC_TOPN - 1, n_blk))
    own_b = jnp.broadcast_to(own[:, None, None], (t, KV_HEADS, 1))
    idx = jnp.concatenate([top, own_b], axis=-1)
    blk_ok = jnp.concatenate([top < own[:, None, None], jnp.ones(own_b.shape, bool)], axis=-1)
    g_idx = jnp.arange(KV_HEADS)[None, :, None]
    ks = k_blk.transpose(2, 0, 1, 3)[g_idx, idx]
    vs = v_blk.transpose(2, 0, 1, 3)[g_idx, idx]
    n_sel = idx.shape[-1]
    kpos = idx[..., None] * SLC_BLOCK + jnp.arange(SLC_BLOCK)
    ls = jnp.einsum('tgrd,tgsbd->tgrsb', qg, ks).astype(jnp.float32) * ATT_SCALE
    ls = ls + table[t5_bucket(q_pos[:, None, None, None, None] - kpos[:, :, None]), head[..., None]]
    ms = (blk_ok[..., None] & (kpos <= q_pos[:, None, None, None]))[:, :, None]
    ps = masked_softmax(ls.reshape(t, KV_HEADS, GROUP, -1), ms.reshape(t, KV_HEADS, 1, -1))
    o_slc = jnp.einsum('tgrk,tgkd->tgrd', ps.astype(vs.dtype),
                       vs.reshape(t, KV_HEADS, n_sel * SLC_BLOCK, HEAD_DIM))
    dist = qp - win_pos
    lw = jnp.einsum('tgrd,wgd->tgrw', qg, k_win).astype(jnp.float32) * ATT_SCALE
    pw = masked_softmax(lw + table[t5_bucket(dist), head],
                        (dist >= 0) & (dist < WINDOW) & (win_pos >= 0))
    o_win = jnp.einsum('tgrw,wgd->tgrd', pw.astype(v_win.dtype), v_win)
    g = gates.reshape(t, KV_HEADS, GROUP, 3)
    o = g[..., 0:1] * o_cmp + g[..., 1:2] * o_slc + g[..., 2:3] * o_win
    return o.reshape(t, N_HEADS, HEAD_DIM)


def nsa_prompt(q, gates, kc, vc, ks, vs, kw, vw, cmp, table):
    b, s = q.shape[:2]
    k_cmp, v_cmp = jax.vmap(nsa_compress, in_axes=(0, 0, None))(kc, vc, cmp)
    kb, vb = to_blocks(ks, SLC_BLOCK), to_blocks(vs, SLC_BLOCK)
    pad = ((0, 0), (WINDOW, 0), (0, 0), (0, 0))
    kwp, vwp = jnp.pad(kw, pad), jnp.pad(vw, pad)
    span = WINDOW + Q_BLOCK
    def block(bi, start):
        wk = lax.dynamic_slice_in_dim(kwp[bi], start, span, axis=0)
        wv = lax.dynamic_slice_in_dim(vwp[bi], start, span, axis=0)
        return nsa_core(q_block(q, bi, start), start + jnp.arange(Q_BLOCK), q_block(gates, bi, start),
                        k_cmp[bi], v_cmp[bi], kb[bi], vb[bi], wk, wv,
                        start - WINDOW + jnp.arange(span), table)
    return sweep_query_blocks(block, b, s)


def nsa_sample(q, gates, kc, vc, ks, vs, kw, vw, q_pos, pool_ck, pool_cv, pool_sk, pool_sv,
               win_k, win_v, j, page_table, cmp, table):
    t = q.shape[1]
    n_buf = win_k.shape[1]
    wk_all = jnp.concatenate([win_k, kw], axis=1)
    wv_all = jnp.concatenate([win_v, vw], axis=1)
    win_pos = PAST_LEN - n_buf + jnp.arange(n_buf + t)
    def one(args):
        qb, gb, kcn, vcn, ksn, vsn, wk, wv, pt = args
        k_cmp, v_cmp = nsa_compress(jnp.concatenate([gather_past(pool_ck, j, pt), kcn], axis=0),
                                    jnp.concatenate([gather_past(pool_cv, j, pt), vcn], axis=0), cmp)
        kb = to_blocks(jnp.concatenate([gather_past(pool_sk, j, pt), ksn], axis=0), SLC_BLOCK)
        vb = to_blocks(jnp.concatenate([gather_past(pool_sv, j, pt), vsn], axis=0), SLC_BLOCK)
        return nsa_core(qb, q_pos, gb, k_cmp, v_cmp, kb, vb, wk, wv, win_pos, table)
    o = lax.map(one, (q, gates, kc, vc, ks, vs, wk_all, wv_all, page_table))
    return o, wk_all[:, t:], wv_all[:, t:]


def mla_project(h, pos, w_in, q_norm, w_qb, kv_norm, qn_gain, qr_gain, kr_gain):
    qa, ckv, kr, z = jnp.split(h @ w_in, MLA_SPLITS, axis=-1)
    lead = h.shape[:-1]
    qh = (rms_norm(qa, q_norm) @ w_qb).reshape(*lead, MLA_HEADS, NOPE_DIM + ROPE_DIM)
    q_nope = rms_norm(qh[..., :NOPE_DIM], qn_gain)
    q_rope = rope(rms_norm(qh[..., NOPE_DIM:], qr_gain), pos[:, None])
    k_rope = rope(rms_norm(kr, kr_gain), pos)
    return q_nope, q_rope, rms_norm(ckv, kv_norm), k_rope, z


def mla_core(q_nope, q_rope, q_pos, k_nope, k_rope, v):
    logits = (jnp.einsum('thd,lhd->thl', q_nope, k_nope)
              + jnp.einsum('thd,ld->thl', q_rope, k_rope)).astype(jnp.float32) * MLA_SCALE
    p = masked_softmax(logits, jnp.arange(k_nope.shape[0])[None, None, :] <= q_pos[:, None, None])
    return jnp.einsum('thl,lhd->thd', p.astype(v.dtype), v)


def mla_prompt(q_nope, q_rope, ckv, k_rope, w_kvb):
    b, s = q_nope.shape[:2]
    kv = (ckv @ w_kvb).reshape(b, s, MLA_HEADS, NOPE_DIM + V_DIM)
    k_nope, v = kv[..., :NOPE_DIM], kv[..., NOPE_DIM:]
    def block(bi, start):
        return mla_core(q_block(q_nope, bi, start), q_block(q_rope, bi, start),
                        start + jnp.arange(Q_BLOCK), k_nope[bi], k_rope[bi], v[bi])
    return sweep_query_blocks(block, b, s)


def mla_sample(q_nope, q_rope, ckv, k_rope, q_pos, pool_ckv, pool_krope, j, page_table, w_kvb):
    w = w_kvb.reshape(KV_LORA, MLA_HEADS, NOPE_DIM + V_DIM)
    w_uk, w_uv = w[..., :NOPE_DIM], w[..., NOPE_DIM:]
    def one(args):
        qn, qr, c_new, kr_new, pt = args
        c = jnp.concatenate([gather_past(pool_ckv, j, pt), c_new], axis=0)
        kr = jnp.concatenate([gather_past(pool_krope, j, pt), kr_new], axis=0)
        q_lat = jnp.einsum('thd,rhd->thr', qn, w_uk)
        logits = (jnp.einsum('thr,lr->thl', q_lat, c)
                  + jnp.einsum('thd,ld->thl', qr, kr)).astype(jnp.float32) * MLA_SCALE
        p = masked_softmax(logits, jnp.arange(c.shape[0])[None, None, :] <= q_pos[:, None, None])
        o_lat = jnp.einsum('thl,lr->thr', p.astype(c.dtype), c)
        return jnp.einsum('thr,rhd->thd', o_lat, w_uv)
    return lax.map(one, (q_nope, q_rope, ckv, k_rope, page_table))


def setup_inputs(seed: int = 0) -> dict:
    n_pages = PAST_LEN // PAGE_SIZE
    n_pool = (DEC_BATCH * n_pages * 5) // 4
    n_buf = min(WINDOW, PAST_LEN)
    keys = iter(jax.random.split(jax.random.key(seed), 64))
    def nrm(shape, scale=1.0):
        return scale * jax.random.normal(next(keys), shape, jnp.float32)
    def lin(*shape):
        return nrm(shape, shape[-2] ** -0.5)
    def gain(*shape):
        return 1.0 + nrm(shape, 0.05)
    kv_pool = (n_pool, PAGE_SIZE, KV_HEADS, HEAD_DIM)
    perm = jax.random.permutation(next(keys), n_pool)
    page_table = perm[: DEC_BATCH * n_pages].reshape(DEC_BATCH, n_pages).astype(jnp.int32)
    return {
        'x_prompt': nrm((BATCH, SEQ, D_MODEL)),
        'x_sample': nrm((DEC_BATCH, DEC_SEQ, D_MODEL)),
        'cache_moba_k': nrm((N_MOBA,) + kv_pool),
        'cache_moba_v': nrm((N_MOBA,) + kv_pool),
        'cache_nsa_cmp_k': nrm((N_NSA,) + kv_pool),
        'cache_nsa_cmp_v': nrm((N_NSA,) + kv_pool),
        'cache_nsa_slc_k': nrm((N_NSA,) + kv_pool),
        'cache_nsa_slc_v': nrm((N_NSA,) + kv_pool),
        'state_nsa_win_k': nrm((N_NSA, DEC_BATCH, n_buf, KV_HEADS, HEAD_DIM)),
        'state_nsa_win_v': nrm((N_NSA, DEC_BATCH, n_buf, KV_HEADS, HEAD_DIM)),
        'cache_mla_ckv': nrm((N_MLA, n_pool, PAGE_SIZE, KV_LORA)),
        'cache_mla_krope': nrm((N_MLA, n_pool, PAGE_SIZE, ROPE_DIM)),
        'page_table': page_table,
        'p_prompt': nrm((DEPTH, BATCH, SEQ, D_PLE)),
        'p_sample': nrm((DEPTH, DEC_BATCH, DEC_SEQ, D_PLE)),
        'rel_bias_table': nrm((N_BUCKETS, N_HEADS), 0.5),
        'norm_gain': gain(DEPTH, D_MODEL),
        'ple_proj': lin(DEPTH, D_PLE, D_MODEL),
        'ple_gate': lin(DEPTH, D_MODEL, D_MODEL),
        'ple_norm': gain(DEPTH, D_MODEL),
        'moba_w_in': lin(N_MOBA, D_MODEL, MOBA_IN),
        'moba_q_gain': gain(N_MOBA, HEAD_DIM),
        'moba_k_gain': gain(N_MOBA, HEAD_DIM),
        'moba_w_out': lin(N_MOBA, ATT_WIDTH, D_MODEL),
        'nsa_w_in': lin(N_NSA, D_MODEL, NSA_IN),
        'nsa_q_gain': gain(N_NSA, HEAD_DIM),
        'nsa_kc_gain': gain(N_NSA, HEAD_DIM),
        'nsa_ks_gain': gain(N_NSA, HEAD_DIM),
        'nsa_kw_gain': gain(N_NSA, HEAD_DIM),
        'nsa_cmp_pos': nrm((N_NSA, CMP_LEN, HEAD_DIM), 0.1),
        'nsa_cmp_w1k': lin(N_NSA, CMP_LEN * HEAD_DIM, CMP_HIDDEN),
        'nsa_cmp_w2k': lin(N_NSA, CMP_HIDDEN, HEAD_DIM),
        'nsa_cmp_w1v': lin(N_NSA, CMP_LEN * HEAD_DIM, CMP_HIDDEN),
        'nsa_cmp_w2v': lin(N_NSA, CMP_HIDDEN, HEAD_DIM),
        'nsa_w_out': lin(N_NSA, ATT_WIDTH, D_MODEL),
        'mla_w_in': lin(N_MLA, D_MODEL, MLA_IN),
        'mla_q_norm': gain(N_MLA, Q_LORA),
        'mla_w_qb': lin(N_MLA, Q_LORA, MLA_HEADS * (NOPE_DIM + ROPE_DIM)),
        'mla_kv_norm': gain(N_MLA, KV_LORA),
        'mla_w_kvb': lin(N_MLA, KV_LORA, MLA_HEADS * (NOPE_DIM + V_DIM)),
        'mla_qn_gain': gain(N_MLA, NOPE_DIM),
        'mla_qr_gain': gain(N_MLA, ROPE_DIM),
        'mla_kr_gain': gain(N_MLA, ROPE_DIM),
        'mla_w_out': lin(N_MLA, MLA_HEADS * V_DIM, D_MODEL),
    }


def reference(x_prompt, x_sample,
              cache_moba_k, cache_moba_v,
              cache_nsa_cmp_k, cache_nsa_cmp_v, cache_nsa_slc_k, cache_nsa_slc_v,
              state_nsa_win_k, state_nsa_win_v,
              cache_mla_ckv, cache_mla_krope,
              page_table, p_prompt, p_sample,
              rel_bias_table, norm_gain, ple_proj, ple_gate, ple_norm,
              moba_w_in, moba_q_gain, moba_k_gain, moba_w_out,
              nsa_w_in, nsa_q_gain, nsa_kc_gain, nsa_ks_gain, nsa_kw_gain,
              nsa_cmp_pos, nsa_cmp_w1k, nsa_cmp_w2k, nsa_cmp_w1v, nsa_cmp_w2v, nsa_w_out,
              mla_w_in, mla_q_norm, mla_w_qb, mla_kv_norm, mla_w_kvb,
              mla_qn_gain, mla_qr_gain, mla_kr_gain, mla_w_out):
    pos_p = jnp.arange(x_prompt.shape[1], dtype=jnp.int32)
    pos_s = PAST_LEN + jnp.arange(x_sample.shape[1], dtype=jnp.int32)
    n_win_p = min(WINDOW, x_prompt.shape[1])
    xp, xs = x_prompt, x_sample
    new = {}
    for i in range(DEPTH):
        kind, j = i % N_MIXERS, i // N_MIXERS
        hp, hs = rms_norm(xp, norm_gain[i]), rms_norm(xs, norm_gain[i])
        if kind == 0:
            qp, kp, vp, zp = moba_project(hp, moba_w_in[j], moba_q_gain[j], moba_k_gain[j])
            qs, ks, vs, zs = moba_project(hs, moba_w_in[j], moba_q_gain[j], moba_k_gain[j])
            op = moba_prompt(qp, kp, vp, rel_bias_table)
            os_ = moba_sample(qs, ks, vs, pos_s, cache_moba_k, cache_moba_v, j, page_table, rel_bias_table)
            w_out = moba_w_out[j]
            rows = {'moba_k_prompt': kp, 'moba_v_prompt': vp, 'moba_k_sample': ks, 'moba_v_sample': vs}
        elif kind == 1:
            cmp = (nsa_cmp_pos[j], nsa_cmp_w1k[j], nsa_cmp_w2k[j], nsa_cmp_w1v[j], nsa_cmp_w2v[j], nsa_kc_gain[j])
            qp, gp, kcp, vcp, ksp, vsp, kwp, vwp, zp = nsa_project(hp, nsa_w_in[j], nsa_q_gain[j], nsa_ks_gain[j], nsa_kw_gain[j])
            qs, gs, kcs, vcs, kss, vss, kws, vws, zs = nsa_project(hs, nsa_w_in[j], nsa_q_gain[j], nsa_ks_gain[j], nsa_kw_gain[j])
            op = nsa_prompt(qp, gp, kcp, vcp, ksp, vsp, kwp, vwp, cmp, rel_bias_table)
            os_, wk_new, wv_new = nsa_sample(qs, gs, kcs, vcs, kss, vss, kws, vws, pos_s,
                                             cache_nsa_cmp_k, cache_nsa_cmp_v, cache_nsa_slc_k, cache_nsa_slc_v,
                                             state_nsa_win_k[j], state_nsa_win_v[j], j, page_table, cmp, rel_bias_table)
            w_out = nsa_w_out[j]
            rows = {'nsa_cmp_k_prompt': kcp, 'nsa_cmp_v_prompt': vcp, 'nsa_slc_k_prompt': ksp, 'nsa_slc_v_prompt': vsp,
                    'nsa_win_k_prompt': kwp[:, kwp.shape[1] - n_win_p:], 'nsa_win_v_prompt': vwp[:, vwp.shape[1] - n_win_p:],
                    'nsa_cmp_k_sample': kcs, 'nsa_cmp_v_sample': vcs, 'nsa_slc_k_sample': kss, 'nsa_slc_v_sample': vss,
                    'nsa_win_k_sample': wk_new, 'nsa_win_v_sample': wv_new}
        else:
            mp = (mla_w_in[j], mla_q_norm[j], mla_w_qb[j], mla_kv_norm[j], mla_qn_gain[j], mla_qr_gain[j], mla_kr_gain[j])
            qnp, qrp, cp, krp, zp = mla_project(hp, pos_p, *mp)
            qns, qrs, cs, krs, zs = mla_project(hs, pos_s, *mp)
            op = mla_prompt(qnp, qrp, cp, krp, mla_w_kvb[j])
            os_ = mla_sample(qns, qrs, cs, krs, pos_s, cache_mla_ckv, cache_mla_krope, j, page_table, mla_w_kvb[j])
            w_out = mla_w_out[j]
            rows = {'mla_ckv_prompt': cp, 'mla_krope_prompt': krp, 'mla_ckv_sample': cs, 'mla_krope_sample': krs}
        for name, r in rows.items():
            new.setdefault(name, []).append(r)
        xp = add_ple(xp + gated_out(op, zp, w_out), p_prompt[i], ple_proj[i], ple_gate[i], ple_norm[i])
        xs = add_ple(xs + gated_out(os_, zs, w_out), p_sample[i], ple_proj[i], ple_gate[i], ple_norm[i])
    stack = lambda name: jnp.stack(new[name])
    return (xp, xs,
            stack('moba_k_prompt'), stack('moba_v_prompt'), stack('moba_k_sample'), stack('moba_v_sample'),
            stack('nsa_cmp_k_prompt'), stack('nsa_cmp_v_prompt'), stack('nsa_slc_k_prompt'), stack('nsa_slc_v_prompt'),
            stack('nsa_win_k_prompt'), stack('nsa_win_v_prompt'),
            stack('nsa_cmp_k_sample'), stack('nsa_cmp_v_sample'), stack('nsa_slc_k_sample'), stack('nsa_slc_v_sample'),
            stack('nsa_win_k_sample'), stack('nsa_win_v_sample'),
            stack('mla_ckv_prompt'), stack('mla_krope_prompt'), stack('mla_ckv_sample'), stack('mla_krope_sample'))
```

```python
import functools
import math

import numpy as np
import jax
import jax.numpy as jnp
from jax import lax
from jax.experimental import pallas as pl
from jax.experimental.pallas import tpu as pltpu

F32 = jnp.float32
BF16 = jnp.bfloat16
I32 = jnp.int32

D_MODEL = 1024
D_PLE = 256
HEAD_DIM = 64
N_HEADS = 16
KV_HEADS = 4
GROUP = 4
ATT_WIDTH = 1024
KV_WIDTH = 256
EPS = 1e-6
NEG = -1e30
ATT_SCALE = HEAD_DIM ** -0.5
LANE = 128
PAGE = 128

N_BUCKETS = 32
MAX_EXACT = 16
MAX_DISTANCE = 2048
FAR_DIST = 1513

MOBA_BLOCK = 256
MOBA_TOPK = 3

CMP_STRIDE = 16
CMP_LEN = 32
CMP_HIDDEN = 128
SLC_BLOCK = 64
SLC_TOPN = 16
WINDOW = 512

Q_LORA = 384
KV_LORA = 256
NOPE = 64
ROPE = 32
V_DIM = 64
ROPE_THETA = 10000.0
MLA_SCALE = (NOPE + ROPE) ** -0.5

TQ = 128
TK = 256
TQ_MLA = 256
N_BIAS_CLS = -(-(FAR_DIST + TK - 1) // TQ)
VMEM_LIMIT = 56 * 1024 * 1024


def _cparams(sem):
    return pltpu.CompilerParams(dimension_semantics=sem, vmem_limit_bytes=VMEM_LIMIT)


def _dot(a, b):
    return lax.dot_general(a, b, (((1,), (0,)), ((), ())), preferred_element_type=F32)


def _dot_nt(a, b):
    return lax.dot_general(a, b, (((1,), (1,)), ((), ())), preferred_element_type=F32)


def _split(x):
    hi = x.astype(BF16)
    lo = (x - hi.astype(F32)).astype(BF16)
    return hi, lo


def _dot3_nt(a, b):
    ah, al = _split(a)
    bh, bl = _split(b)
    return _dot_nt(ah, bh) + _dot_nt(ah, bl) + _dot_nt(al, bh)


def _dot_hl(a, b_exact):
    ah, al = _split(a)
    return _dot(ah, b_exact) + _dot(al, b_exact)


def _rms_rows(x, gain):
    ms = jnp.mean(x * x, axis=-1, keepdims=True)
    return x * lax.rsqrt(ms + EPS) * gain


def _lane_iota(shape):
    return lax.broadcasted_iota(I32, shape, len(shape) - 1)


def _row_iota(shape):
    return lax.broadcasted_iota(I32, shape, len(shape) - 2)


def _pad_head(y, h):
    blk = y[:, LANE * (h // 2):LANE * (h // 2) + LANE]
    if h % 2:
        blk = pltpu.roll(blk, HEAD_DIM, axis=1)
    return jnp.where(_lane_iota(blk.shape) < HEAD_DIM, blk, 0.0)


def _even_odd(y, h):
    blk = y[:, LANE * (h // 2):LANE * (h // 2) + LANE]
    rolled = pltpu.roll(blk, HEAD_DIM, axis=1)
    lo_half = _lane_iota(blk.shape) < HEAD_DIM
    if h % 2:
        return jnp.where(lo_half, rolled, 0.0), jnp.where(lo_half, 0.0, blk)
    return jnp.where(lo_half, blk, 0.0), jnp.where(lo_half, 0.0, rolled)


def _head_inv(xh, n):
    return lax.rsqrt(jnp.sum(xh * xh, axis=-1, keepdims=True) * (1.0 / n) + EPS)


def _compact_scale(invs):
    cols = []
    rows = invs[0].shape[0]
    lo_half = _lane_iota((rows, LANE)) < HEAD_DIM
    for c in range(len(invs) // 2):
        cols.append(jnp.where(lo_half, invs[2 * c], invs[2 * c + 1]))
    return jnp.concatenate(cols, axis=1) if len(cols) > 1 else cols[0]


def _silu(z):
    return z * jax.nn.sigmoid(z)


def _bias_eval_kernel(tab_ref, d_ref, o_ref):
    d = d_ref[...]
    n = jnp.maximum(d, 0)
    nf = jnp.maximum(n, 1).astype(F32)
    big = MAX_EXACT + (jnp.log(nf / MAX_EXACT) / math.log(MAX_DISTANCE / MAX_EXACT)
                       * (N_BUCKETS - MAX_EXACT)).astype(I32)
    bucket = jnp.where(n < MAX_EXACT, n, jnp.minimum(big, N_BUCKETS - 1))
    for h in range(N_HEADS):
        acc = jnp.zeros(d.shape, F32)
        for b in range(N_BUCKETS):
            acc = jnp.where(bucket == b, tab_ref[b * N_HEADS + h], acc)
        o_ref[h] = acc


def _bias_eval(table, dist):
    dist = jnp.asarray(dist, I32)
    r, c = dist.shape
    tr = 128 if r % 128 == 0 else r
    return pl.pallas_call(
        _bias_eval_kernel,
        out_shape=jax.ShapeDtypeStruct((N_HEADS, r, c), F32),
        grid=(r // tr,),
        in_specs=[pl.BlockSpec(memory_space=pltpu.SMEM),
                  pl.BlockSpec((tr, c), lambda i: (i, 0))],
        out_specs=pl.BlockSpec((N_HEADS, tr, c), lambda i: (0, i, 0)),
        compiler_params=_cparams(("arbitrary",)),
        name="bias_eval",
    )(table.reshape(-1), dist)


def _out_ple_kernel(o_ref, z_ref, x_ref, p_ref, wo_ref, wp_ref, wg_ref, gn_ref, y_ref):
    a = (o_ref[...] * _silu(z_ref[...])).astype(BF16)
    x1 = x_ref[...] + _dot(a, wo_ref[...])
    hn = _rms_rows(x1, gn_ref[...]).astype(BF16)
    gate = jax.nn.sigmoid(_dot(hn, wg_ref[...]))
    pp = _dot(p_ref[...].astype(BF16), wp_ref[...])
    y_ref[...] = x1 + gate * pp


def _out_ple(o, z, x, p, w_out, w_proj, w_gate, g_norm):
    t = x.shape[0]
    tm = min(256, t)
    row = lambda w: pl.BlockSpec((tm, w), lambda i: (i, 0))
    full = lambda a: pl.BlockSpec(a.shape, lambda i: (0, 0))
    wo, wp, wg = w_out.astype(BF16), w_proj.astype(BF16), w_gate.astype(BF16)
    gn = g_norm.reshape(1, D_MODEL)
    return pl.pallas_call(
        _out_ple_kernel,
        out_shape=jax.ShapeDtypeStruct((t, D_MODEL), F32),
        grid=(t // tm,),
        in_specs=[row(D_MODEL), row(D_MODEL), row(D_MODEL), row(D_PLE), full(wo), full(wp), full(wg), full(gn)],
        out_specs=row(D_MODEL),
        compiler_params=_cparams(("parallel",)),
        name="out_ple",
    )(o, z, x, p, wo, wp, wg, gn)


def _flash(groups, tk, bounds, bias_far, bias_near, mask_fn, out_w):
    n0, n_far, n_diag, n_end = bounds
    rows = sum(g[0].shape[0] for g in groups)

    def step(n, carry, mode):
        m, l, acc = carry
        start = pl.multiple_of(n * tk, tk)
        s = jnp.concatenate([_dot_nt(q, k_ref[pl.ds(start, tk), :]) for q, k_ref, _ in groups], axis=0)
        if mode == "far":
            if bias_far is not None:
                s = s + bias_far
        elif bias_near is not None:
            s = s + bias_near(n)
        if mode == "diag" and mask_fn is not None:
            s = jnp.where(mask_fn(n), s, NEG)
        m_new = jnp.maximum(m, jnp.max(s, axis=-1, keepdims=True))
        alpha = jnp.exp(m - m_new)
        p = jnp.exp(s - m_new)
        l = alpha * l + jnp.sum(p, axis=-1, keepdims=True)
        pb = p.astype(BF16)
        pv, r0 = [], 0
        for q, _, v_ref in groups:
            r = q.shape[0]
            pv.append(_dot(pb[r0:r0 + r], v_ref[pl.ds(start, tk), :]))
            r0 += r
        acc = alpha * acc + jnp.concatenate(pv, axis=0)
        return m_new, l, acc

    carry = (jnp.full((rows, 1), NEG, F32), jnp.zeros((rows, 1), F32), jnp.zeros((rows, out_w), F32))
    carry = lax.fori_loop(n0, n_far, functools.partial(step, mode="far"), carry)
    carry = lax.fori_loop(n_far, n_diag, functools.partial(step, mode="near"), carry)
    carry = lax.fori_loop(n_diag, n_end, functools.partial(step, mode="diag"), carry)
    _, l, acc = carry
    return acc / jnp.maximum(l, 1e-30)


def _row_heads_const(vals, tq):
    rows = len(vals) * tq
    ri = lax.broadcasted_iota(I32, (rows, 1), 0)
    out = jnp.full((rows, 1), vals[-1], F32)
    for r in range(len(vals) - 2, -1, -1):
        out = jnp.where(ri < (r + 1) * tq, vals[r], out)
    return out


def _dist_tile(q0, k0, tq, nheads, tk):
    shape = (nheads * tq, tk)
    t = lax.broadcasted_iota(I32, shape, 0)
    t = t - (t // tq) * tq
    return (q0 + t) - (k0 + lax.broadcasted_iota(I32, shape, 1))


def _moba_proj_kernel(x_ref, g_ref, w_ref, qg_ref, kg_ref, kgc_ref,
                      q_ref, k_ref, v_ref, z_ref, kaug_ref, veo_ref, km_ref, *, tm):
    i = pl.program_id(1)
    h = _rms_rows(x_ref[...], g_ref[...]).astype(BF16)
    y = _dot(h, w_ref[...])
    yq, yk, yv = y[:, :ATT_WIDTH], y[:, ATT_WIDTH:ATT_WIDTH + KV_WIDTH], y[:, ATT_WIDTH + KV_WIDTH:ATT_WIDTH + 2 * KV_WIDTH]
    for hh in range(N_HEADS):
        xh = _pad_head(yq, hh)
        q_ref[hh] = xh * _head_inv(xh, HEAD_DIM) * qg_ref[...]
    lane = _lane_iota((tm, LANE))
    blk = (i * tm + _row_iota((tm, LANE))) // MOBA_BLOCK
    onehot = jnp.where((lane - HEAD_DIM) == blk, 1.0, 0.0)
    invs = []
    for g in range(KV_HEADS):
        kh = _pad_head(yk, g)
        inv = _head_inv(kh, HEAD_DIM)
        invs.append(inv)
        kaug_ref[g] = jnp.where(lane < HEAD_DIM, kh * inv * kg_ref[...], onehot).astype(BF16)
        ve, vo = _even_odd(yv, g)
        veo_ref[g, 0] = ve.astype(BF16)
        veo_ref[g, 1] = vo.astype(BF16)
    kc = yk * _compact_scale(invs) * kgc_ref[...]
    k_ref[...] = kc
    km_ref[...] = jnp.sum(kc, axis=0, keepdims=True) * (1.0 / tm)
    v_ref[...] = yv
    z_ref[...] = y[:, ATT_WIDTH + 2 * KV_WIDTH:]


def _moba_proj(x, gain, w_in, q_gain, k_gain):
    b, s, _ = x.shape
    tm = min(MOBA_BLOCK, s)
    nt = s // tm
    w = w_in.astype(BF16)
    pad = lambda g: jnp.pad(g, (0, LANE - HEAD_DIM)).reshape(1, LANE)
    kgc = jnp.tile(k_gain, KV_HEADS).reshape(1, KV_WIDTH)
    full = lambda a: pl.BlockSpec(a.shape, lambda bi, i: (0,) * a.ndim)
    gain = gain.reshape(1, D_MODEL)
    qg, kg = pad(q_gain), pad(k_gain)
    outs = pl.pallas_call(
        functools.partial(_moba_proj_kernel, tm=tm),
        out_shape=(
            jax.ShapeDtypeStruct((b, N_HEADS, s, LANE), F32),
            jax.ShapeDtypeStruct((b, s, KV_WIDTH), F32),
            jax.ShapeDtypeStruct((b, s, KV_WIDTH), F32),
            jax.ShapeDtypeStruct((b, s, ATT_WIDTH), F32),
            jax.ShapeDtypeStruct((b, KV_HEADS, s, LANE), BF16),
            jax.ShapeDtypeStruct((b, KV_HEADS, 2, s, LANE), BF16),
            jax.ShapeDtypeStruct((b, nt, 1, KV_WIDTH), F32),
        ),
        grid=(b, nt),
        in_specs=[pl.BlockSpec((None, tm, D_MODEL), lambda bi, i: (bi, i, 0)),
                  full(gain), full(w), full(qg), full(kg), full(kgc)],
        out_specs=(
            pl.BlockSpec((None, N_HEADS, tm, LANE), lambda bi, i: (bi, 0, i, 0)),
            pl.BlockSpec((None, tm, KV_WIDTH), lambda bi, i: (bi, i, 0)),
            pl.BlockSpec((None, tm, KV_WIDTH), lambda bi, i: (bi, i, 0)),
            pl.BlockSpec((None, tm, ATT_WIDTH), lambda bi, i: (bi, i, 0)),
            pl.BlockSpec((None, KV_HEADS, tm, LANE), lambda bi, i: (bi, 0, i, 0)),
            pl.BlockSpec((None, KV_HEADS, 2, tm, LANE), lambda bi, i: (bi, 0, 0, i, 0)),
            pl.BlockSpec((None, None, 1, KV_WIDTH), lambda bi, i: (bi, i, 0, 0)),
        ),
        compiler_params=_cparams(("parallel", "parallel")),
        name="moba_proj",
    )(x, gain, w, qg, kg, kgc)
    return outs


def _top_lanes(score, valid, k):
    lane = _lane_iota(score.shape).astype(F32)
    g = jnp.where(valid, score, NEG)
    sel = jnp.zeros(score.shape, jnp.bool_)
    for _ in range(k):
        m = jnp.max(g, axis=-1, keepdims=True)
        cand = jnp.logical_and(g == m, valid)
        idx = jnp.min(jnp.where(cand, lane, 1e9), axis=-1, keepdims=True)
        pick = jnp.logical_and(lane == idx, m > 0.5 * NEG)
        sel = jnp.logical_or(sel, pick)
        g = jnp.where(pick, NEG, g)
    return sel


def _moba_attn_kernel(cfar_ref, q_ref, kaug_ref, veo_ref, km_ref, tt_ref, o_ref, *, tq, tk):
    g = pl.program_id(1)
    qi = pl.program_id(2)
    rows = GROUP * tq
    q = q_ref[...].reshape(rows, LANE)
    gate = _dot3_nt(q, km_ref[...])
    lane = _lane_iota((rows, LANE))
    own = (qi * tq) // MOBA_BLOCK
    valid = jnp.logical_and(lane >= HEAD_DIM, lane < HEAD_DIM + own)
    sel = _top_lanes(gate, valid, MOBA_TOPK)
    keep = jnp.logical_or(sel, lane == HEAD_DIM + own)
    qa = jnp.where(lane < HEAD_DIM, q * ATT_SCALE, jnp.where(keep, 0.0, NEG)).astype(BF16)
    cfar = _row_heads_const([cfar_ref[g * GROUP + r] for r in range(GROUP)], tq)

    def near(n):
        cls = qi - (tk // tq) * n
        return jnp.concatenate([tt_ref[r, cls] for r in range(GROUP)], axis=0)

    q0 = qi * tq
    n_diag = q0 // tk
    n_far = jnp.minimum(jnp.maximum(qi - N_BIAS_CLS + (tk // tq), 0) // (tk // tq), n_diag)
    groups = [(qa[r * tq:(r + 1) * tq], kaug_ref, veo_ref.at[r % 2]) for r in range(GROUP)]
    o = _flash(groups, tk, (0, n_far, n_diag, n_diag + 1), cfar, near,
               lambda n: _dist_tile(q0, n * tk, tq, GROUP, tk) >= 0, LANE)
    o_ref[...] = jnp.concatenate([o[0:tq] + o[tq:2 * tq], o[2 * tq:3 * tq] + o[3 * tq:4 * tq]], axis=1)


def _moba_attn(q_hm, kaug, veo, km, tt, cfar):
    b, _, s, _ = q_hm.shape
    tq, tk = TQ, TK
    nq = s // tq
    return pl.pallas_call(
        functools.partial(_moba_attn_kernel, tq=tq, tk=tk),
        out_shape=jax.ShapeDtypeStruct((b, s, ATT_WIDTH), F32),
        grid=(b, KV_HEADS, nq),
        in_specs=[
            pl.BlockSpec(memory_space=pltpu.SMEM),
            pl.BlockSpec((None, GROUP, tq, LANE), lambda bi, g, qi: (bi, g, qi, 0)),
            pl.BlockSpec((None, None, s, LANE), lambda bi, g, qi: (bi, g, 0, 0)),
            pl.BlockSpec((None, None, 2, s, LANE), lambda bi, g, qi: (bi, g, 0, 0, 0)),
            pl.BlockSpec((None, None, LANE, LANE), lambda bi, g, qi: (bi, g, 0, 0)),
            pl.BlockSpec((GROUP, N_BIAS_CLS, tq, tk), lambda bi, g, qi: (g, 0, 0, 0)),
        ],
        out_specs=pl.BlockSpec((None, tq, 2 * LANE), lambda bi, g, qi: (bi, qi, g)),
        compiler_params=_cparams(("parallel", "parallel", "arbitrary")),
        name="moba_attn",
    )(cfar, q_hm, kaug, veo, km, tt)


def _moba_gate_keys(km, nb):
    b = km.shape[0]
    kmh = km.reshape(b, nb, KV_HEADS, HEAD_DIM).transpose(0, 2, 1, 3)
    out = jnp.zeros((b, KV_HEADS, LANE, LANE), F32)
    return out.at[:, :, HEAD_DIM:HEAD_DIM + nb, :HEAD_DIM].set(kmh)


def _tile_bias_dist():
    cls = np.arange(N_BIAS_CLS)[:, None, None] * TQ
    t = np.arange(TQ)[None, :, None]
    j = np.arange(TK)[None, None, :]
    return (cls + t - j).reshape(N_BIAS_CLS * TQ, TK).astype(np.int32)


NSA_KV6 = 6 * KV_WIDTH
NSA_Z0 = ATT_WIDTH + NSA_KV6
NSA_G0 = NSA_Z0 + ATT_WIDTH
NSA_COLS = NSA_G0 + LANE


def _nsa_weight(w_in):
    g0 = ATT_WIDTH + NSA_KV6
    gates = w_in[:, g0:g0 + 3 * N_HEADS].reshape(D_MODEL, N_HEADS, 3).transpose(0, 2, 1).reshape(D_MODEL, 3 * N_HEADS)
    gates = jnp.pad(gates, ((0, 0), (0, LANE - 3 * N_HEADS)))
    return jnp.concatenate([w_in[:, :g0], w_in[:, g0 + 3 * N_HEADS:], gates], axis=1).astype(BF16)


def _nsa_proj_kernel(x_ref, g_ref, w_ref, qg_ref, ksg_ref, kwg_ref, ksgc_ref, kwgc_ref,
                     q_ref, kv6_ref, z_ref, gt_ref, kaug_ref, vseo_ref, kw_ref, vweo_ref, *, tm):
    i = pl.program_id(1)
    h = _rms_rows(x_ref[...], g_ref[...]).astype(BF16)
    y = _dot(h, w_ref[...])
    yq = y[:, :ATT_WIDTH]
    seg = lambda j: y[:, ATT_WIDTH + j * KV_WIDTH:ATT_WIDTH + (j + 1) * KV_WIDTH]
    for hh in range(N_HEADS):
        xh = _pad_head(yq, hh)
        q_ref[hh] = xh * _head_inv(xh, HEAD_DIM) * qg_ref[...]
    lane = _lane_iota((tm, LANE))
    blk = (i * tm + _row_iota((tm, LANE))) // SLC_BLOCK
    onehot = jnp.where(lane == blk, 1.0, 0.0).astype(BF16)
    ks, vs, kw, vw = seg(2), seg(3), seg(4), seg(5)
    inv_s, inv_w = [], []
    for g in range(KV_HEADS):
        kh = _pad_head(ks, g)
        inv = _head_inv(kh, HEAD_DIM)
        inv_s.append(inv)
        kaug_ref[g] = jnp.concatenate([(kh * inv * ksg_ref[...]).astype(BF16), onehot], axis=1)
        kh = _pad_head(kw, g)
        inv = _head_inv(kh, HEAD_DIM)
        inv_w.append(inv)
        kw_ref[g] = (kh * inv * kwg_ref[...]).astype(BF16)
        ve, vo = _even_odd(vs, g)
        vseo_ref[g, 0] = ve.astype(BF16)
        vseo_ref[g, 1] = vo.astype(BF16)
        ve, vo = _even_odd(vw, g)
        vweo_ref[g, 0] = ve.astype(BF16)
        vweo_ref[g, 1] = vo.astype(BF16)
    kv6_ref[...] = jnp.concatenate(
        [seg(0), seg(1), ks * _compact_scale(inv_s) * ksgc_ref[...], vs,
         kw * _compact_scale(inv_w) * kwgc_ref[...], vw], axis=1)
    z_ref[...] = y[:, NSA_Z0:NSA_G0]
    gt_ref[...] = jax.nn.sigmoid(y[:, NSA_G0:])


def _nsa_proj(x, gain, w, q_gain, ks_gain, kw_gain):
    b, s, _ = x.shape
    tm = min(256, s)
    nt = s // tm
    pad = lambda g: jnp.pad(g, (0, LANE - HEAD_DIM)).reshape(1, LANE)
    til = lambda g: jnp.tile(g, KV_HEADS).reshape(1, KV_WIDTH)
    full = lambda a: pl.BlockSpec(a.shape, lambda bi, i: (0,) * a.ndim)
    gain = gain.reshape(1, D_MODEL)
    args = (gain, w, pad(q_gain), pad(ks_gain), pad(kw_gain), til(ks_gain), til(kw_gain))
    hm = lambda n, wd: pl.BlockSpec((None, n, tm, wd), lambda bi, i: (bi, 0, i, 0))
    eo = pl.BlockSpec((None, KV_HEADS, 2, tm, LANE), lambda bi, i: (bi, 0, 0, i, 0))
    row = lambda wd: pl.BlockSpec((None, tm, wd), lambda bi, i: (bi, i, 0))
    return pl.pallas_call(
        functools.partial(_nsa_proj_kernel, tm=tm),
        out_shape=(
            jax.ShapeDtypeStruct((b, N_HEADS, s, LANE), F32),
            jax.ShapeDtypeStruct((b, s, NSA_KV6), F32),
            jax.ShapeDtypeStruct((b, s, ATT_WIDTH), F32),
            jax.ShapeDtypeStruct((b, s, LANE), F32),
            jax.ShapeDtypeStruct((b, KV_HEADS, s, 2 * LANE), BF16),
            jax.ShapeDtypeStruct((b, KV_HEADS, 2, s, LANE), BF16),
            jax.ShapeDtypeStruct((b, KV_HEADS, s, LANE), BF16),
            jax.ShapeDtypeStruct((b, KV_HEADS, 2, s, LANE), BF16),
        ),
        grid=(b, nt),
        in_specs=[row(D_MODEL)] + [full(a) for a in args],
        out_specs=(hm(N_HEADS, LANE), row(NSA_KV6), row(ATT_WIDTH), row(LANE),
                   hm(KV_HEADS, 2 * LANE), eo, hm(KV_HEADS, LANE), eo),
        compiler_params=_cparams(("parallel", "parallel")),
        name="nsa_proj",
    )(x, *args)


def _gelu_tanh(x):
    return 0.5 * x * (1.0 + jnp.tanh(math.sqrt(2.0 / math.pi) * (x + 0.044715 * (x * x * x))))


def _compress(buf_ref, pe_ref, w1_ref, w2_ref, nc):
    def body(p, acc):
        pe = pe_ref[pl.ds(p, 1), :]
        w = w1_ref[p]
        return tuple(a + _dot((buf_ref[c, pl.ds(p, nc, stride=CMP_STRIDE), :] + pe).astype(BF16), w)
                     for c, a in enumerate(acc))
    zero = jnp.zeros((nc, 2 * CMP_HIDDEN), F32)
    hid = lax.fori_loop(0, CMP_LEN, body, (zero, zero))
    return jnp.concatenate([_dot(_gelu_tanh(h).astype(BF16), w2_ref[...]) for h in hid], axis=1)


def _norm_heads_padded(y, gain_pad, nheads):
    cols = []
    for g in range(nheads):
        blk = y[:, g * LANE:(g + 1) * LANE]
        cols.append(blk * _head_inv(blk, HEAD_DIM) * gain_pad)
    return jnp.concatenate(cols, axis=1)


def _cmp_weights(w1, w2):
    eye = jnp.eye(2, dtype=F32)
    w1r = w1.reshape(CMP_LEN, HEAD_DIM, CMP_HIDDEN)
    w1bd = jnp.einsum("pdj,gh->pgdhj", w1r, eye).reshape(CMP_LEN, LANE, 2 * CMP_HIDDEN)
    w2p = jnp.pad(w2, ((0, 0), (0, LANE - HEAD_DIM)))
    w2bd = jnp.einsum("je,gh->gjhe", w2p, eye).reshape(2 * CMP_HIDDEN, 2 * LANE)
    return w1bd.astype(BF16), w2bd.astype(BF16)


def _nsa_cmp_prompt_kernel(x_ref, pe_ref, w1_ref, w2_ref, kcg_ref, o_ref, buf_ref, *, s, nc):
    kv = pl.program_id(1)
    for c in range(2):
        buf_ref[c, pl.ds(0, s), :] = x_ref[:, c * LANE:(c + 1) * LANE]
        buf_ref[c, pl.ds(s, CMP_STRIDE), :] = jnp.zeros((CMP_STRIDE, LANE), F32)
    y = _compress(buf_ref, pe_ref, w1_ref, w2_ref, nc)
    o_ref[...] = jnp.where(kv == 0, _norm_heads_padded(y, kcg_ref[...], KV_HEADS), y)


def _nsa_cmp_prompt(kv6, pe, w1bd2, w2bd2, kcg_pad):
    b, s, _ = kv6.shape
    nc = s // CMP_STRIDE
    return pl.pallas_call(
        functools.partial(_nsa_cmp_prompt_kernel, s=s, nc=nc),
        out_shape=jax.ShapeDtypeStruct((b, 2, nc, KV_HEADS * LANE), F32),
        grid=(b, 2),
        in_specs=[
            pl.BlockSpec((None, s, KV_WIDTH), lambda bi, kv: (bi, 0, kv)),
            pl.BlockSpec(pe.shape, lambda bi, kv: (0, 0)),
            pl.BlockSpec((None,) + w1bd2.shape[1:], lambda bi, kv: (kv, 0, 0, 0)),
            pl.BlockSpec((None,) + w2bd2.shape[1:], lambda bi, kv: (kv, 0, 0)),
            pl.BlockSpec(kcg_pad.shape, lambda bi, kv: (0, 0)),
        ],
        out_specs=pl.BlockSpec((None, None, nc, KV_HEADS * LANE), lambda bi, kv: (bi, kv, 0, 0)),
        scratch_shapes=[pltpu.VMEM((2, s + CMP_STRIDE, LANE), F32)],
        compiler_params=_cparams(("parallel", "arbitrary")),
        name="nsa_cmp_prompt",
    )(kv6, pe, w1bd2, w2bd2, kcg_pad)


def _top_sublanes(score, valid, k):
    row = _row_iota(score.shape).astype(F32)
    g = jnp.where(valid, score, NEG)
    sel = jnp.zeros(score.shape, jnp.bool_)
    for _ in range(k):
        m = jnp.max(g, axis=0, keepdims=True)
        cand = jnp.logical_and(g == m, valid)
        idx = jnp.min(jnp.where(cand, row, 1e9), axis=0, keepdims=True)
        pick = jnp.logical_and(row == idx, m > 0.5 * NEG)
        sel = jnp.logical_or(sel, pick)
        g = jnp.where(pick, NEG, g)
    return sel


def _nsa_attn_kernel(cfar_ref, q_ref, cmp_ref, tb_ref, ovt_ref, kaug_ref, vseo_ref, kw_ref, vweo_ref,
                     tt_ref, gt_ref, o_ref, *, tq, tk, nc, nq):
    g = pl.program_id(1)
    qi = pl.program_id(2)
    rows = GROUP * tq
    q0 = qi * tq
    q = q_ref[...].reshape(rows, LANE)
    lane = _lane_iota((rows, LANE))
    qs = jnp.where(lane < HEAD_DIM, q * ATT_SCALE, 0.0).astype(BF16)

    kc = cmp_ref[0].astype(BF16)
    vc = cmp_ref[1].astype(BF16)
    ci = _row_iota((nc, tq))
    ti = _lane_iota((nc, tq))
    cmask = (CMP_STRIDE * ci + CMP_LEN - 1 - q0) <= ti
    tb0 = pl.multiple_of(8 * (nq - 1) - 8 * qi, 8)
    pcsum = jnp.zeros((nc, tq), F32)
    o_cmp = []
    for r in range(GROUP):
        lc = _dot_nt(kc, qs[r * tq:(r + 1) * tq]) + tb_ref[r, pl.ds(tb0, nc), :]
        lc = jnp.where(cmask, lc, NEG)
        mx = jnp.max(lc, axis=0, keepdims=True)
        e = jnp.where(cmask, jnp.exp(lc - mx), 0.0)
        pc = e / jnp.maximum(jnp.sum(e, axis=0, keepdims=True), 1e-30)
        pcsum = pcsum + pc
        o_cmp.append(_dot(pc.T.astype(BF16), vc))
    o_cmp = jnp.concatenate(o_cmp, axis=0)

    imp = _dot_hl_left(ovt_ref[...], pcsum)
    ni = _row_iota((LANE, tq))
    own = (q0 + _lane_iota((LANE, tq))) // SLC_BLOCK
    sel = _top_sublanes(imp, ni < own, SLC_TOPN - 1)
    keep = jnp.logical_or(sel, ni == own)
    selb = jnp.where(keep, 0.0, NEG).T
    qa = jnp.concatenate([qs, jnp.concatenate([selb] * GROUP, axis=0).astype(BF16)], axis=1)

    cfar = _row_heads_const([cfar_ref[g * GROUP + r] for r in range(GROUP)], tq)
    ratio = tk // tq

    def near(n):
        cls = qi - ratio * n
        return jnp.concatenate([tt_ref[r, cls] for r in range(GROUP)], axis=0)

    n_diag = q0 // tk
    n_far = jnp.minimum(jnp.maximum(qi - N_BIAS_CLS + ratio, 0) // ratio, n_diag)
    groups = [(qa[r * tq:(r + 1) * tq], kaug_ref, vseo_ref.at[r % 2]) for r in range(GROUP)]
    o_slc = _flash(groups, tk, (0, n_far, n_diag, n_diag + 1), cfar, near,
                   lambda n: _dist_tile(q0, n * tk, tq, GROUP, tk) >= 0, LANE)

    def wmask(n):
        d = _dist_tile(q0, n * tk, tq, GROUP, tk)
        return jnp.logical_and(d >= 0, d < WINDOW)

    n_lo = jnp.maximum(q0 - WINDOW, 0) // tk
    groups = [(qs[r * tq:(r + 1) * tq], kw_ref, vweo_ref.at[r % 2]) for r in range(GROUP)]
    o_win = _flash(groups, tk, (n_lo, n_lo, n_lo, n_diag + 1), None, near, wmask, LANE)

    gt = gt_ref[...]
    glane = _lane_iota(gt.shape)
    def gcol(c):
        cols = [jnp.sum(jnp.where(glane == c * N_HEADS + g * GROUP + r, gt, 0.0), axis=-1, keepdims=True)
                for r in range(GROUP)]
        return jnp.concatenate(cols, axis=0)
    hi = lax.broadcasted_iota(I32, (rows, LANE), 0) // tq
    odd = (hi - (hi // 2) * 2) == 1
    o_c = gcol(0) * o_cmp
    o_c = jnp.where(odd, pltpu.roll(o_c, HEAD_DIM, axis=1), o_c)
    o = o_c + gcol(1) * o_slc + gcol(2) * o_win
    o_ref[...] = jnp.concatenate([o[0:tq] + o[tq:2 * tq], o[2 * tq:3 * tq] + o[3 * tq:4 * tq]], axis=1)


def _dot_hl_left(a_exact, b):
    bh, bl = _split(b)
    return _dot(a_exact, bh) + _dot(a_exact, bl)


def _nsa_attn(q_hm, cmp, tb, ovt, kaug, vseo, kwp, vweo, tt, gates, cfar):
    b, _, s, _ = q_hm.shape
    tq, tk = TQ, TK
    nq = s // tq
    nc = s // CMP_STRIDE
    hm = lambda wd: pl.BlockSpec((None, None, s, wd), lambda bi, g, qi: (bi, g, 0, 0))
    eo = pl.BlockSpec((None, None, 2, s, LANE), lambda bi, g, qi: (bi, g, 0, 0, 0))
    return pl.pallas_call(
        functools.partial(_nsa_attn_kernel, tq=tq, tk=tk, nc=nc, nq=nq),
        out_shape=jax.ShapeDtypeStruct((b, s, ATT_WIDTH), F32),
        grid=(b, KV_HEADS, nq),
        in_specs=[
            pl.BlockSpec(memory_space=pltpu.SMEM),
            pl.BlockSpec((None, GROUP, tq, LANE), lambda bi, g, qi: (bi, g, qi, 0)),
            pl.BlockSpec((None, 2, nc, LANE), lambda bi, g, qi: (bi, 0, 0, g)),
            pl.BlockSpec((GROUP,) + tb.shape[1:], lambda bi, g, qi: (g, 0, 0)),
            pl.BlockSpec(ovt.shape, lambda bi, g, qi: (0, 0)),
            hm(2 * LANE), eo, hm(LANE), eo,
            pl.BlockSpec((GROUP, N_BIAS_CLS, tq, tk), lambda bi, g, qi: (g, 0, 0, 0)),
            pl.BlockSpec((None, tq, LANE), lambda bi, g, qi: (bi, qi, 0)),
        ],
        out_specs=pl.BlockSpec((None, tq, 2 * LANE), lambda bi, g, qi: (bi, qi, g)),
        compiler_params=_cparams(("parallel", "parallel", "arbitrary")),
        name="nsa_attn",
    )(cfar, q_hm, cmp, tb, ovt, kaug, vseo, kwp, vweo, tt, gates)


def _cmp_bias_dist(nq, nc):
    mmax = 8 * (nq - 1)
    m = mmax - np.arange(mmax + nc)[:, None]
    t = np.arange(TQ)[None, :]
    return (CMP_STRIDE * m + t - (CMP_LEN - 1)).astype(np.int32)


def _overlap_t(nc, nblk):
    c = np.arange(nc)[None, :]
    n = np.arange(LANE)[:, None]
    ov = (CMP_STRIDE * c < SLC_BLOCK * n + SLC_BLOCK) & (CMP_STRIDE * c + CMP_LEN - 1 >= SLC_BLOCK * n) & (n < nblk)
    return ov.astype(np.float32)


MLA_Z0 = Q_LORA + KV_LORA
MLA_R0 = MLA_Z0 + ATT_WIDTH
MLA_COLS = MLA_R0 + LANE
HALF = ROPE // 2


def _mla_weights(w_in, w_qb, w_kvb):
    r0 = Q_LORA + KV_LORA
    w = jnp.concatenate([w_in[:, :r0], w_in[:, r0 + ROPE:], jnp.pad(w_in[:, r0:r0 + ROPE], ((0, 0), (0, LANE - ROPE)))], axis=1)
    wq = jnp.pad(w_qb.reshape(Q_LORA, N_HEADS, NOPE + ROPE), ((0, 0), (0, 0), (0, LANE - NOPE - ROPE)))
    wq = wq.reshape(Q_LORA, N_HEADS * LANE)
    wkv = w_kvb.reshape(KV_LORA, N_HEADS, NOPE + V_DIM)
    wk = jnp.pad(wkv[..., :NOPE], ((0, 0), (0, 0), (0, LANE - NOPE)))
    eye = jnp.pad(jnp.eye(ROPE, dtype=F32), ((0, LANE - ROPE), (NOPE, LANE - NOPE - ROPE)))
    wk = jnp.concatenate([wk, jnp.broadcast_to(eye[:, None, :], (LANE, N_HEADS, LANE))], axis=0)
    wk = wk.reshape(KV_LORA + LANE, N_HEADS * LANE)
    wv = wkv[..., NOPE:]
    odd = (jnp.arange(N_HEADS) % 2 == 1)[None, :, None]
    wv = jnp.where(odd, jnp.pad(wv, ((0, 0), (0, 0), (LANE - V_DIM, 0))), jnp.pad(wv, ((0, 0), (0, 0), (0, LANE - V_DIM))))
    wv = wv.reshape(KV_LORA, N_HEADS * LANE)
    return w.astype(BF16), wq.astype(BF16), wk.astype(BF16), wv.astype(BF16)


def _rope_lanes(x, cos, sin, first):
    lane = _lane_iota(x.shape)
    lower = lane < first + HALF
    partner = jnp.where(lower, pltpu.roll(x, LANE - HALF, axis=1), pltpu.roll(x, HALF, axis=1))
    return x * cos + jnp.where(lower, -partner, partner) * sin


def _mla_proj_kernel(x_ref, g_ref, w_ref, qn_ref, wq_ref, kvn_ref, qg_ref, krg_ref, cos_ref, sin_ref,
                     wk_ref, wv_ref, q_ref, c_ref, kr_ref, z_ref, kcat_ref, v_ref, *, tm):
    h = _rms_rows(x_ref[...], g_ref[...]).astype(BF16)
    y = _dot(h, w_ref[...])
    qa = _rms_rows(y[:, :Q_LORA], qn_ref[...]).astype(BF16)
    qh = _dot(qa, wq_ref[...])
    cosk, sink = cos_ref[...], sin_ref[...]
    lane = _lane_iota((tm, LANE))
    is_nope = lane < NOPE
    is_rope = jnp.logical_and(lane >= NOPE, lane < NOPE + ROPE)
    cosq = jnp.where(is_nope, 1.0, jnp.where(is_rope, pltpu.roll(cosk, NOPE, axis=1), 0.0))
    sinq = jnp.where(is_rope, pltpu.roll(sink, NOPE, axis=1), 0.0)
    for hh in range(N_HEADS):
        xh = qh[:, hh * LANE:(hh + 1) * LANE]
        sq = xh * xh
        inv_n = lax.rsqrt(jnp.sum(jnp.where(is_nope, sq, 0.0), axis=-1, keepdims=True) * (1.0 / NOPE) + EPS)
        inv_r = lax.rsqrt(jnp.sum(jnp.where(is_rope, sq, 0.0), axis=-1, keepdims=True) * (1.0 / ROPE) + EPS)
        xn = xh * jnp.where(is_nope, inv_n, inv_r) * qg_ref[...]
        q_ref[hh] = (_rope_lanes(xn, cosq, sinq, NOPE) * MLA_SCALE).astype(BF16)
    c = _rms_rows(y[:, Q_LORA:Q_LORA + KV_LORA], kvn_ref[...])
    c_ref[...] = c
    kr = y[:, MLA_R0:]
    kr = kr * lax.rsqrt(jnp.sum(kr * kr, axis=-1, keepdims=True) * (1.0 / ROPE) + EPS) * krg_ref[...]
    kr = _rope_lanes(kr, cosk, sink, 0)
    kr_ref[...] = kr
    z_ref[...] = y[:, MLA_Z0:MLA_R0]
    cb = c.astype(BF16)
    kcat = _dot(jnp.concatenate([cb, kr.astype(BF16)], axis=1), wk_ref[...])
    vv = _dot(cb, wv_ref[...])
    for hh in range(N_HEADS):
        kcat_ref[hh] = kcat[:, hh * LANE:(hh + 1) * LANE].astype(BF16)
        v_ref[hh] = vv[:, hh * LANE:(hh + 1) * LANE].astype(BF16)


def _mla_proj(x, gain, ws, q_norm, kv_norm, qn_gain, qr_gain, kr_gain, cosk, sink):
    b, s, _ = x.shape
    w, wq, wk, wv = ws
    tm = min(256, s)
    nt = s // tm
    full = lambda a: pl.BlockSpec(a.shape, lambda bi, i: (0,) * a.ndim)
    row = lambda wd: pl.BlockSpec((None, tm, wd), lambda bi, i: (bi, i, 0))
    hm = pl.BlockSpec((None, N_HEADS, tm, LANE), lambda bi, i: (bi, 0, i, 0))
    qg = jnp.pad(jnp.concatenate([qn_gain, qr_gain]), (0, LANE - NOPE - ROPE)).reshape(1, LANE)
    krg = jnp.pad(kr_gain, (0, LANE - ROPE)).reshape(1, LANE)
    args = (gain.reshape(1, D_MODEL), w, q_norm.reshape(1, Q_LORA), wq, kv_norm.reshape(1, KV_LORA), qg, krg)
    hm_shape = lambda dt: jax.ShapeDtypeStruct((b, N_HEADS, s, LANE), dt)
    return pl.pallas_call(
        functools.partial(_mla_proj_kernel, tm=tm),
        out_shape=(hm_shape(BF16), jax.ShapeDtypeStruct((b, s, KV_LORA), F32), jax.ShapeDtypeStruct((b, s, LANE), F32),
                   jax.ShapeDtypeStruct((b, s, ATT_WIDTH), F32), hm_shape(BF16), hm_shape(BF16)),
        grid=(b, nt),
        in_specs=[row(D_MODEL)] + [full(a) for a in args]
                 + [pl.BlockSpec((None, tm, LANE), lambda bi, i: (bi, i, 0))] * 2 + [full(wk), full(wv)],
        out_specs=(hm, row(KV_LORA), row(LANE), row(ATT_WIDTH), hm, hm),
        compiler_params=_cparams(("parallel", "parallel")),
        name="mla_proj",
    )(x, *args, cosk, sink, wk, wv)


def _rope_tables(pos):
    freq = ROPE_THETA ** (-jnp.arange(HALF, dtype=F32) / HALF)
    ang = pos[..., None].astype(F32) * freq
    two = lambda a: jnp.pad(jnp.concatenate([a, a], axis=-1), ((0, 0), (0, 0), (0, LANE - ROPE)))
    return two(jnp.cos(ang)), two(jnp.sin(ang))


def _mla_attn_kernel(q_ref, k_ref, v_ref, o_ref, *, tq, tk):
    qi = pl.program_id(2)
    q0 = qi * tq
    groups = [(q_ref[r], k_ref.at[r], v_ref.at[r]) for r in range(2)]
    ratio = tq // tk
    o = _flash(groups, tk, (0, 0, qi * ratio, (qi + 1) * ratio), None, None,
               lambda n: _dist_tile(q0, n * tk, tq, 2, tk) >= 0, LANE)
    o_ref[...] = o[0:tq] + o[tq:2 * tq]


def _mla_attn(q_hm, kcat, v128):
    b, _, s, _ = q_hm.shape
    tq = min(TQ_MLA, s)
    tk = min(TK, tq)
    pair = lambda rows: pl.BlockSpec((None, 2, rows, LANE), lambda bi, hp, qi: (bi, hp, (qi if rows == tq else 0), 0))
    return pl.pallas_call(
        functools.partial(_mla_attn_kernel, tq=tq, tk=tk),
        out_shape=jax.ShapeDtypeStruct((b, s, ATT_WIDTH), F32),
        grid=(b, N_HEADS // 2, s // tq),
        in_specs=[pl.BlockSpec((None, 2, tq, LANE), lambda bi, hp, qi: (bi, hp, qi, 0)),
                  pl.BlockSpec((None, 2, s, LANE), lambda bi, hp, qi: (bi, hp, 0, 0)),
                  pl.BlockSpec((None, 2, s, LANE), lambda bi, hp, qi: (bi, hp, 0, 0))],
        out_specs=pl.BlockSpec((None, tq, LANE), lambda bi, hp, qi: (bi, qi, hp)),
        compiler_params=_cparams(("parallel", "parallel", "arbitrary")),
        name="mla_attn",
    )(q_hm, kcat, v128)


SAMPLE_CHUNK = 1024


def _page_copies(pt_ref, s, cache_ref, j, dst_ref, sem, npg):
    return [pltpu.make_async_copy(cache_ref.at[j, pt_ref[s, p]], dst_ref.at[pl.ds(p * PAGE, PAGE)], sem)
            for p in range(npg)]


def _prefetch_step(s, n, start_fn, wait_fn):
    slot = s % 2

    @pl.when(s == 0)
    def _():
        start_fn(s, slot)

    @pl.when(s + 1 < n)
    def _():
        start_fn(s + 1, 1 - slot)

    wait_fn(s, slot)
    return slot


def _masked_single_query(qs, k_ref, v_ref, bias, keep, s_new, v_new, s_scr, past):
    nch = past // SAMPLE_CHUNK
    for c in range(nch):
        kc = k_ref[pl.ds(c * SAMPLE_CHUNK, SAMPLE_CHUNK), :].astype(BF16)
        s_scr[:, c * SAMPLE_CHUNK:(c + 1) * SAMPLE_CHUNK] = _dot_nt(qs, kc)
    s = s_scr[...] + bias
    if keep is not None:
        s = jnp.where(keep, s, NEG)
    m = jnp.maximum(jnp.max(s, axis=-1, keepdims=True), s_new)
    p = jnp.exp(s - m)
    p_new = jnp.exp(s_new - m)
    l = jnp.sum(p, axis=-1, keepdims=True) + p_new
    pb = p.astype(BF16)
    acc = p_new * v_new
    for c in range(nch):
        vc = v_ref[pl.ds(c * SAMPLE_CHUNK, SAMPLE_CHUNK), :].astype(BF16)
        acc = acc + _dot(pb[:, c * SAMPLE_CHUNK:(c + 1) * SAMPLE_CHUNK], vc)
    return acc / jnp.maximum(l, 1e-30)


def _moba_sample_kernel(pt_ref, qbd_ref, kn_ref, vn_ref, bias_ref, exp_ref, ck_ref, cv_ref, o_ref,
                        kbuf, vbuf, s_scr, km_scr, sem, *, j, npg, ndb):
    s = pl.program_id(0)
    past = npg * PAGE
    nblk = past // MOBA_BLOCK

    def start(step, slot):
        for c in _page_copies(pt_ref, step, ck_ref, j, kbuf.at[slot], sem.at[0, slot], npg):
            c.start()
        for c in _page_copies(pt_ref, step, cv_ref, j, vbuf.at[slot], sem.at[1, slot], npg):
            c.start()

    def wait(step, slot):
        for c in _page_copies(pt_ref, step, ck_ref, j, kbuf.at[slot], sem.at[0, slot], npg):
            c.wait()
        for c in _page_copies(pt_ref, step, cv_ref, j, vbuf.at[slot], sem.at[1, slot], npg):
            c.wait()

    slot = _prefetch_step(s, ndb, start, wait)
    kb, vb = kbuf.at[slot], vbuf.at[slot]
    qf = qbd_ref[...]
    qs = (qf * ATT_SCALE).astype(BF16)
    for n in range(nblk):
        km_scr[pl.ds(n, 1), :] = jnp.sum(kb[pl.ds(n * MOBA_BLOCK, MOBA_BLOCK), :], axis=0, keepdims=True) * (1.0 / MOBA_BLOCK)
    gate = _dot3_nt(qf, km_scr[...])
    sel = _top_lanes(gate, jnp.ones(gate.shape, jnp.bool_), MOBA_TOPK)
    keep = _dot(jnp.where(sel, 1.0, 0.0).astype(BF16), exp_ref[...]) > 0.5
    bias = bias_ref[...]
    s_new = jnp.sum(qf * ATT_SCALE * kn_ref[...], axis=-1, keepdims=True) + bias[:, past:past + 1]
    o_ref[...] = _masked_single_query(qs, kb, vb, bias[:, :past], keep, s_new, vn_ref[...], s_scr, past)


def _sample_specs(ndb):
    per = lambda r, w: pl.BlockSpec((None, r, w), lambda s, pt: (s, 0, 0))
    full = lambda a: pl.BlockSpec(a.shape, lambda s, pt: (0,) * a.ndim)
    return per, full


def _moba_sample(page_table, qbd, k_new, v_new, bias_s, expand, cache_k, cache_v, j):
    ndb, npg = page_table.shape
    past = npg * PAGE
    per, full = _sample_specs(ndb)
    anyspec = pl.BlockSpec(memory_space=pl.ANY)
    return pl.pallas_call(
        functools.partial(_moba_sample_kernel, j=j, npg=npg, ndb=ndb),
        out_shape=jax.ShapeDtypeStruct((ndb, N_HEADS, KV_WIDTH), F32),
        grid_spec=pltpu.PrefetchScalarGridSpec(
            num_scalar_prefetch=1,
            grid=(ndb,),
            in_specs=[per(N_HEADS, KV_WIDTH), per(1, KV_WIDTH), per(1, KV_WIDTH), full(bias_s), full(expand), anyspec, anyspec],
            out_specs=per(N_HEADS, KV_WIDTH),
            scratch_shapes=[pltpu.VMEM((2, past, KV_WIDTH), F32), pltpu.VMEM((2, past, KV_WIDTH), F32),
                            pltpu.VMEM((N_HEADS, past), F32), pltpu.VMEM((past // MOBA_BLOCK, KV_WIDTH), F32),
                            pltpu.SemaphoreType.DMA((2, 2))],
        ),
        compiler_params=_cparams(("arbitrary",)),
        name="moba_sample",
    )(page_table, qbd, k_new, v_new, bias_s, expand, cache_k, cache_v)


def _half_page_copies(pt_ref, s, cache_ref, j, dst_ref, sem, npg):
    return [pltpu.make_async_copy(cache_ref.at[j, pt_ref[s, p], :, pl.ds(c * LANE, LANE)],
                                  dst_ref.at[c, pl.ds(p * PAGE, PAGE)], sem)
            for p in range(npg) for c in range(2)]


def _nsa_cmp_sample_kernel(pt_ref, qp_ref, kn_ref, vn_ref, pe_ref, kcg_ref, bias_ref, ov_ref, gsum_ref,
                           w1_ref, w2_ref, ck_ref, cv_ref, o_ref, sel_ref, kbuf, vbuf, sem, *, j, npg, ndb):
    s = pl.program_id(0)
    past = npg * PAGE
    nc = past // CMP_STRIDE
    kcopies = lambda step: _half_page_copies(pt_ref, step, ck_ref, j, kbuf, sem.at[0], npg)
    vcopies = lambda step: _half_page_copies(pt_ref, step, cv_ref, j, vbuf, sem.at[1], npg)

    @pl.when(s == 0)
    def _():
        for c in kcopies(s) + vcopies(s):
            c.start()

    def put_tail(buf, new):
        first = _row_iota((CMP_STRIDE, LANE)) == 0
        for c in range(2):
            row = jnp.broadcast_to(new[:, c * LANE:(c + 1) * LANE], (CMP_STRIDE, LANE))
            buf[c, pl.ds(past, CMP_STRIDE), :] = jnp.where(first, row, 0.0)

    for c in kcopies(s):
        c.wait()
    put_tail(kbuf, kn_ref[...])
    kcmp = _norm_heads_padded(_compress(kbuf, pe_ref, w1_ref.at[0], w2_ref.at[0], nc), kcg_ref[...], KV_HEADS)

    @pl.when(s + 1 < ndb)
    def _():
        for c in kcopies(s + 1):
            c.start()

    for c in vcopies(s):
        c.wait()
    put_tail(vbuf, vn_ref[...])
    vcmp = _compress(vbuf, pe_ref, w1_ref.at[1], w2_ref.at[1], nc)

    @pl.when(s + 1 < ndb)
    def _():
        for c in vcopies(s + 1):
            c.start()

    qs = (qp_ref[...] * ATT_SCALE).astype(BF16)
    lc = _dot_nt(qs, kcmp.astype(BF16)) + bias_ref[...]
    cmask = (CMP_STRIDE * _lane_iota(lc.shape) + CMP_LEN - 1) <= past
    lc = jnp.where(cmask, lc, NEG)
    mx = jnp.max(lc, axis=-1, keepdims=True)
    e = jnp.where(cmask, jnp.exp(lc - mx), 0.0)
    pc = e / jnp.maximum(jnp.sum(e, axis=-1, keepdims=True), 1e-30)
    o_ref[...] = _dot(pc.astype(BF16), vcmp.astype(BF16))
    imp = _dot_hl(_dot_hl_left(gsum_ref[...], pc), ov_ref[...])
    own = past // SLC_BLOCK
    sel = _top_lanes(imp, _lane_iota(imp.shape) < own, SLC_TOPN - 1)
    sel_ref[...] = jnp.where(sel, 1.0, 0.0)


def _nsa_cmp_sample(page_table, qp, kc_new, vc_new, pe, kcg_pad, bias_c, ov, gsum, cache_k, cache_v, w1bd2, w2bd2, j):
    ndb, npg = page_table.shape
    past = npg * PAGE
    per, full = _sample_specs(ndb)
    anyspec = pl.BlockSpec(memory_space=pl.ANY)
    return pl.pallas_call(
        functools.partial(_nsa_cmp_sample_kernel, j=j, npg=npg, ndb=ndb),
        out_shape=(jax.ShapeDtypeStruct((ndb, N_HEADS, KV_HEADS * LANE), F32),
                   jax.ShapeDtypeStruct((ndb, 8, LANE), F32)),
        grid_spec=pltpu.PrefetchScalarGridSpec(
            num_scalar_prefetch=1,
            grid=(ndb,),
            in_specs=[per(N_HEADS, KV_HEADS * LANE), per(1, KV_WIDTH), per(1, KV_WIDTH), full(pe), full(kcg_pad),
                      full(bias_c), full(ov), full(gsum), full(w1bd2), full(w2bd2), anyspec, anyspec],
            out_specs=(per(N_HEADS, KV_HEADS * LANE), per(8, LANE)),
            scratch_shapes=[pltpu.VMEM((2, past + CMP_STRIDE, LANE), F32), pltpu.VMEM((2, past + CMP_STRIDE, LANE), F32),
                            pltpu.SemaphoreType.DMA((2,))],
        ),
        compiler_params=_cparams(("arbitrary",)),
        name="nsa_cmp_sample",
    )(page_table, qp, kc_new, vc_new, pe, kcg_pad, bias_c, ov, gsum, w1bd2, w2bd2, cache_k, cache_v)


def _nsa_slc_sample_kernel(pt_ref, qbd_ref, ksn_ref, vsn_ref, kwn_ref, vwn_ref, sel_ref, ocmp_ref, gt_ref,
                           bias_ref, biasw_ref, exp_ref, grow_ref, wk_ref, wv_ref, ck_ref, cv_ref,
                           o_ref, wko_ref, wvo_ref, kbuf, vbuf, s_scr, sem, *, j, npg, ndb, nbuf):
    s = pl.program_id(0)
    past = npg * PAGE

    def start(step, slot):
        for c in _page_copies(pt_ref, step, ck_ref, j, kbuf.at[slot], sem.at[0, slot], npg):
            c.start()
        for c in _page_copies(pt_ref, step, cv_ref, j, vbuf.at[slot], sem.at[1, slot], npg):
            c.start()

    def wait(step, slot):
        for c in _page_copies(pt_ref, step, ck_ref, j, kbuf.at[slot], sem.at[0, slot], npg):
            c.wait()
        for c in _page_copies(pt_ref, step, cv_ref, j, vbuf.at[slot], sem.at[1, slot], npg):
            c.wait()

    slot = _prefetch_step(s, ndb, start, wait)
    qf = qbd_ref[...]
    qsc = qf * ATT_SCALE
    qs = qsc.astype(BF16)
    selrows = _dot(grow_ref[...], sel_ref[...].astype(BF16))
    keep = _dot(selrows.astype(BF16), exp_ref[...]) > 0.5
    bias = bias_ref[...]
    s_new = jnp.sum(qsc * ksn_ref[...], axis=-1, keepdims=True) + bias[:, past:past + 1]
    o_slc = _masked_single_query(qs, kbuf.at[slot], vbuf.at[slot], bias[:, :past], keep, s_new, vsn_ref[...], s_scr, past)

    wk, wv = wk_ref[...], wv_ref[...]
    biasw = biasw_ref[...]
    sw = _dot_nt(qs, wk.astype(BF16)) + biasw[:, :nbuf]
    dist = nbuf - _lane_iota(sw.shape)
    sw = jnp.where(dist < WINDOW, sw, NEG)
    sw_new = jnp.sum(qsc * kwn_ref[...], axis=-1, keepdims=True) + biasw[:, nbuf:nbuf + 1]
    m = jnp.maximum(jnp.max(sw, axis=-1, keepdims=True), sw_new)
    p = jnp.exp(sw - m)
    p_new = jnp.exp(sw_new - m)
    l = jnp.sum(p, axis=-1, keepdims=True) + p_new
    o_win = (_dot(p.astype(BF16), wv.astype(BF16)) + p_new * vwn_ref[...]) / jnp.maximum(l, 1e-30)
    last = _row_iota(wk.shape) == nbuf - 1
    wko_ref[...] = jnp.where(last, jnp.broadcast_to(kwn_ref[...], wk.shape), pltpu.roll(wk, nbuf - 1, axis=0))
    wvo_ref[...] = jnp.where(last, jnp.broadcast_to(vwn_ref[...], wv.shape), pltpu.roll(wv, nbuf - 1, axis=0))

    gt = gt_ref[...]
    o_ref[...] = gt[:, 0:1] * ocmp_ref[...] + gt[:, 1:2] * o_slc + gt[:, 2:3] * o_win


def _nsa_slc_sample(page_table, qbd, ks_new, vs_new, kw_new, vw_new, sel, o_cmp, gates3, bias_s, bias_w,
                    expand, grow, win_k, win_v, cache_k, cache_v, j):
    ndb, npg = page_table.shape
    past = npg * PAGE
    nbuf = win_k.shape[2]
    per, full = _sample_specs(ndb)
    anyspec = pl.BlockSpec(memory_space=pl.ANY)
    state = pl.BlockSpec((None, None, nbuf, KV_WIDTH), lambda s, pt: (j, s, 0, 0))
    return pl.pallas_call(
        functools.partial(_nsa_slc_sample_kernel, j=j, npg=npg, ndb=ndb, nbuf=nbuf),
        out_shape=(jax.ShapeDtypeStruct((ndb, N_HEADS, KV_WIDTH), F32),
                   jax.ShapeDtypeStruct((ndb, nbuf, KV_WIDTH), F32), jax.ShapeDtypeStruct((ndb, nbuf, KV_WIDTH), F32)),
        grid_spec=pltpu.PrefetchScalarGridSpec(
            num_scalar_prefetch=1,
            grid=(ndb,),
            in_specs=[per(N_HEADS, KV_WIDTH)] + [per(1, KV_WIDTH)] * 4 + [per(8, LANE), per(N_HEADS, KV_WIDTH), per(N_HEADS, LANE),
                      full(bias_s), full(bias_w), full(expand), full(grow), state, state,
                      anyspec, anyspec],
            out_specs=(per(N_HEADS, KV_WIDTH), per(nbuf, KV_WIDTH), per(nbuf, KV_WIDTH)),
            scratch_shapes=[pltpu.VMEM((2, past, KV_WIDTH), F32), pltpu.VMEM((2, past, KV_WIDTH), F32),
                            pltpu.VMEM((N_HEADS, past), F32), pltpu.SemaphoreType.DMA((2, 2))],
        ),
        compiler_params=_cparams(("arbitrary",)),
        name="nsa_slc_sample",
    )(page_table, qbd, ks_new, vs_new, kw_new, vw_new, sel, o_cmp, gates3, bias_s, bias_w, expand, grow,
      win_k, win_v, cache_k, cache_v)


def _mla_sample_kernel(pt_ref, ql_ref, qr_ref, cn_ref, krn_ref, cc_ref, ckr_ref, o_ref,
                       cbuf, rbuf, s_scr, sem, *, j, npg, ndb):
    s = pl.program_id(0)
    past = npg * PAGE

    def start(step, slot):
        for c in _page_copies(pt_ref, step, cc_ref, j, cbuf.at[slot], sem.at[0, slot], npg):
            c.start()
        for c in _page_copies(pt_ref, step, ckr_ref, j, rbuf.at[slot], sem.at[1, slot], npg):
            c.start()

    def wait(step, slot):
        for c in _page_copies(pt_ref, step, cc_ref, j, cbuf.at[slot], sem.at[0, slot], npg):
            c.wait()
        for c in _page_copies(pt_ref, step, ckr_ref, j, rbuf.at[slot], sem.at[1, slot], npg):
            c.wait()

    slot = _prefetch_step(s, ndb, start, wait)
    cb, rb = cbuf.at[slot], rbuf.at[slot]
    ql, qr = ql_ref[...], qr_ref[...]
    qlb, qrb = ql.astype(BF16), qr.astype(BF16)
    nch = past // SAMPLE_CHUNK
    for c in range(nch):
        rows = pl.ds(c * SAMPLE_CHUNK, SAMPLE_CHUNK)
        s_scr[:, c * SAMPLE_CHUNK:(c + 1) * SAMPLE_CHUNK] = (
            _dot_nt(qlb, cb[rows, :].astype(BF16)) + _dot_nt(qrb, rb[rows, :].astype(BF16)))
    sc = s_scr[...]
    s_new = jnp.sum(ql * cn_ref[...], axis=-1, keepdims=True) + jnp.sum(qr * krn_ref[...], axis=-1, keepdims=True)
    m = jnp.maximum(jnp.max(sc, axis=-1, keepdims=True), s_new)
    p = jnp.exp(sc - m)
    p_new = jnp.exp(s_new - m)
    l = jnp.sum(p, axis=-1, keepdims=True) + p_new
    pb = p.astype(BF16)
    acc = p_new * cn_ref[...]
    for c in range(nch):
        acc = acc + _dot(pb[:, c * SAMPLE_CHUNK:(c + 1) * SAMPLE_CHUNK],
                         cb[pl.ds(c * SAMPLE_CHUNK, SAMPLE_CHUNK), :].astype(BF16))
    o_ref[...] = acc / jnp.maximum(l, 1e-30)


def _mla_sample(page_table, q_lat, q_rope, c_new, kr_new, cache_c, cache_kr, j):
    ndb, npg = page_table.shape
    past = npg * PAGE
    per, full = _sample_specs(ndb)
    anyspec = pl.BlockSpec(memory_space=pl.ANY)
    return pl.pallas_call(
        functools.partial(_mla_sample_kernel, j=j, npg=npg, ndb=ndb),
        out_shape=jax.ShapeDtypeStruct((ndb, N_HEADS, KV_LORA), F32),
        grid_spec=pltpu.PrefetchScalarGridSpec(
            num_scalar_prefetch=1,
            grid=(ndb,),
            in_specs=[per(N_HEADS, KV_LORA), per(N_HEADS, ROPE), per(1, KV_LORA), per(1, ROPE), anyspec, anyspec],
            out_specs=per(N_HEADS, KV_LORA),
            scratch_shapes=[pltpu.VMEM((2, past, KV_LORA), F32), pltpu.VMEM((2, past, ROPE), F32),
                            pltpu.VMEM((N_HEADS, past), F32), pltpu.SemaphoreType.DMA((2, 2))],
        ),
        compiler_params=_cparams(("arbitrary",)),
        name="mla_sample",
    )(page_table, q_lat, q_rope, c_new, kr_new, cache_c, cache_kr)


def _bmm_kernel(a_ref, b_ref, o_ref):
    o_ref[...] = _dot(a_ref[...].astype(BF16), b_ref[...])


def _bmm(a, b):
    n, m, k = a.shape
    w = b.shape[2]
    return pl.pallas_call(
        _bmm_kernel,
        out_shape=jax.ShapeDtypeStruct((n, m, w), F32),
        grid=(n,),
        in_specs=[pl.BlockSpec((None, m, k), lambda h: (h, 0, 0)), pl.BlockSpec((None, k, w), lambda h: (h, 0, 0))],
        out_specs=pl.BlockSpec((None, m, w), lambda h: (h, 0, 0)),
        compiler_params=_cparams(("parallel",)),
        name="bmm",
    )(a, b.astype(BF16))


_GROUP_OF_HEAD = (np.arange(N_HEADS)[:, None] // GROUP == np.arange(KV_HEADS)[None, :]).astype(np.float32)


def _place_heads(qh, w):
    x = qh[:, :, :w].astype(F32).transpose(1, 0, 2)
    n = x.shape[0]
    return (x[:, :, None, :] * _GROUP_OF_HEAD[None, :, :, None]).reshape(n, N_HEADS, KV_HEADS * w)


def _diag_heads(o_lat, w, keep):
    n = o_lat.shape[0]
    x = o_lat.reshape(n, N_HEADS, KV_HEADS, w) * _GROUP_OF_HEAD[None, :, :, None]
    return jnp.sum(x, axis=2)[..., :keep]


def kernel(x_prompt, x_sample, cache_moba_k, cache_moba_v, cache_nsa_cmp_k, cache_nsa_cmp_v, cache_nsa_slc_k, cache_nsa_slc_v, state_nsa_win_k, state_nsa_win_v, cache_mla_ckv, cache_mla_krope, page_table, p_prompt, p_sample, rel_bias_table, norm_gain, ple_proj, ple_gate, ple_norm, moba_w_in, moba_q_gain, moba_k_gain, moba_w_out, nsa_w_in, nsa_q_gain, nsa_kc_gain, nsa_ks_gain, nsa_kw_gain, nsa_cmp_pos, nsa_cmp_w1k, nsa_cmp_w2k, nsa_cmp_w1v, nsa_cmp_w2v, nsa_w_out, mla_w_in, mla_q_norm, mla_w_qb, mla_kv_norm, mla_w_kvb, mla_qn_gain, mla_qr_gain, mla_kr_gain, mla_w_out):
    b, s, _ = x_prompt.shape
    ndb = x_sample.shape[0]
    npg = page_table.shape[1]
    past = npg * PAGE
    depth = norm_gain.shape[0]
    nbuf = state_nsa_win_k.shape[2]
    assert x_sample.shape[1] == 1 and s % TK == 0 and s // SLC_BLOCK <= LANE and s // MOBA_BLOCK <= LANE // 4
    assert past % SAMPLE_CHUNK == 0 and past // SLC_BLOCK <= LANE and nbuf == WINDOW and past >= WINDOW
    nq, nc, ncs = s // TQ, s // CMP_STRIDE, past // CMP_STRIDE
    page_table = page_table.astype(I32)
    pool = lambda c: c.reshape(c.shape[0], c.shape[1], PAGE, -1)

    table = rel_bias_table
    cfar = table[N_BUCKETS - 1]
    tt = _bias_eval(table, _tile_bias_dist()).reshape(N_HEADS, N_BIAS_CLS, TQ, TK)
    tb = _bias_eval(table, _cmp_bias_dist(nq, nc))
    rows8 = lambda d: np.broadcast_to(d[None, :], (8, d.shape[0])).astype(np.int32)
    bias_s = _bias_eval(table, rows8(past - np.arange(past + LANE)))[:, 0]
    bias_c = _bias_eval(table, rows8(past - (CMP_STRIDE * np.arange(ncs) + CMP_LEN - 1)))[:, 0]
    bias_w = _bias_eval(table, rows8(nbuf - np.arange(nbuf + LANE)))[:, 0]
    lpos = np.arange(past)[None, :]
    expand256 = jnp.asarray(lpos // MOBA_BLOCK == np.arange(past // MOBA_BLOCK)[:, None], BF16)
    expand64 = jnp.asarray(lpos // SLC_BLOCK == np.arange(LANE)[:, None], BF16)
    ovt = jnp.asarray(_overlap_t(nc, s // SLC_BLOCK), BF16)
    ov_s = jnp.asarray(_overlap_t(ncs, LANE).T, BF16)
    gsum = jnp.asarray(np.pad(_GROUP_OF_HEAD.T, ((0, 8 - KV_HEADS), (0, 0))), BF16)
    grow = jnp.asarray(np.pad(_GROUP_OF_HEAD, ((0, 0), (0, 8 - KV_HEADS))), BF16)
    pos_p = jnp.broadcast_to(jnp.arange(s, dtype=I32), (b, s))
    pos_s = jnp.full((1, ndb), past, I32)
    rope_p, rope_s = _rope_tables(pos_p), _rope_tables(pos_s)

    xp = x_prompt
    xs = x_sample.reshape(1, ndb, D_MODEL)
    new = {}
    srow = lambda a: a.reshape(ndb, 1, a.shape[-1])
    for i in range(depth):
        kind, j = i % 3, i // 3
        g_in = norm_gain[i]
        if kind == 0:
            w = moba_w_in[j]
            qp, kp, vp, zp, kaug, veo, km = _moba_proj(xp, g_in, w, moba_q_gain[j], moba_k_gain[j])
            qs, ks, vs, zs, _, _, _ = _moba_proj(xs, g_in, w, moba_q_gain[j], moba_k_gain[j])
            op = _moba_attn(qp, kaug, veo, _moba_gate_keys(km, s // MOBA_BLOCK), tt, cfar)
            o_lat = _moba_sample(page_table, _place_heads(qs[0], HEAD_DIM), srow(ks[0]), srow(vs[0]), bias_s, expand256,
                                 pool(cache_moba_k), pool(cache_moba_v), j)
            os_ = _diag_heads(o_lat, HEAD_DIM, HEAD_DIM).reshape(ndb, ATT_WIDTH)
            w_out = moba_w_out[j]
            rows = {"moba_k_prompt": kp.reshape(b, s, KV_HEADS, HEAD_DIM), "moba_v_prompt": vp.reshape(b, s, KV_HEADS, HEAD_DIM),
                    "moba_k_sample": ks.reshape(ndb, 1, KV_HEADS, HEAD_DIM), "moba_v_sample": vs.reshape(ndb, 1, KV_HEADS, HEAD_DIM)}
        elif kind == 1:
            w = _nsa_weight(nsa_w_in[j])
            gains = (nsa_q_gain[j], nsa_ks_gain[j], nsa_kw_gain[j])
            qp, kv6, zp, gates, kaug, vseo, kwp, vweo = _nsa_proj(xp, g_in, w, *gains)
            qs, skv6, zs, sgates, _, _, _, _ = _nsa_proj(xs, g_in, w, *gains)
            pe = jnp.tile(nsa_cmp_pos[j], (1, 2))
            w1k, w2k = _cmp_weights(nsa_cmp_w1k[j], nsa_cmp_w2k[j])
            w1v, w2v = _cmp_weights(nsa_cmp_w1v[j], nsa_cmp_w2v[j])
            w1bd2, w2bd2 = jnp.stack([w1k, w1v]), jnp.stack([w2k, w2v])
            kcg = jnp.pad(nsa_kc_gain[j], (0, LANE - HEAD_DIM)).reshape(1, LANE)
            cmpo = _nsa_cmp_prompt(kv6, pe, w1bd2, w2bd2, kcg)
            op = _nsa_attn(qp, cmpo, tb, ovt, kaug, vseo, kwp, vweo, tt, gates, cfar)
            seg = lambda a, k: a[..., k * KV_WIDTH:(k + 1) * KV_WIDTH]
            sseg = lambda k: srow(seg(skv6[0], k))
            o_cmp_lat, sel = _nsa_cmp_sample(page_table, _place_heads(qs[0], LANE), sseg(0), sseg(1), pe, kcg, bias_c, ov_s,
                                             gsum, pool(cache_nsa_cmp_k), pool(cache_nsa_cmp_v), w1bd2, w2bd2, j)
            o_cmp = _diag_heads(o_cmp_lat, LANE, HEAD_DIM)
            o_cmp = _place_heads(o_cmp.transpose(1, 0, 2), HEAD_DIM)
            gates3 = sgates[0][:, :3 * N_HEADS].reshape(ndb, 3, N_HEADS).transpose(0, 2, 1)
            gates3 = jnp.pad(gates3, ((0, 0), (0, 0), (0, LANE - 3)))
            o_lat, wk_new, wv_new = _nsa_slc_sample(
                page_table, _place_heads(qs[0], HEAD_DIM), sseg(2), sseg(3), sseg(4), sseg(5), sel, o_cmp, gates3,
                bias_s, bias_w, expand64, grow, state_nsa_win_k.reshape(-1, ndb, nbuf, KV_WIDTH),
                state_nsa_win_v.reshape(-1, ndb, nbuf, KV_WIDTH),
                pool(cache_nsa_slc_k), pool(cache_nsa_slc_v), j)
            os_ = _diag_heads(o_lat, HEAD_DIM, HEAD_DIM).reshape(ndb, ATT_WIDTH)
            w_out = nsa_w_out[j]
            p4 = lambda a: a.reshape(b, -1, KV_HEADS, HEAD_DIM)
            s4 = lambda a: a.reshape(ndb, -1, KV_HEADS, HEAD_DIM)
            n_win_p = min(WINDOW, s)
            rows = {"nsa_cmp_k_prompt": p4(seg(kv6, 0)), "nsa_cmp_v_prompt": p4(seg(kv6, 1)),
                    "nsa_slc_k_prompt": p4(seg(kv6, 2)), "nsa_slc_v_prompt": p4(seg(kv6, 3)),
                    "nsa_win_k_prompt": p4(seg(kv6, 4)[:, s - n_win_p:]), "nsa_win_v_prompt": p4(seg(kv6, 5)[:, s - n_win_p:]),
                    "nsa_cmp_k_sample": s4(seg(skv6[0], 0)), "nsa_cmp_v_sample": s4(seg(skv6[0], 1)),
                    "nsa_slc_k_sample": s4(seg(skv6[0], 2)), "nsa_slc_v_sample": s4(seg(skv6[0], 3)),
                    "nsa_win_k_sample": s4(wk_new), "nsa_win_v_sample": s4(wv_new)}
        else:
            ws = _mla_weights(mla_w_in[j], mla_w_qb[j], mla_w_kvb[j])
            norms = (mla_q_norm[j], mla_kv_norm[j], mla_qn_gain[j], mla_qr_gain[j], mla_kr_gain[j])
            qp, cp, krp, zp, kcat, v128 = _mla_proj(xp, g_in, ws, *norms, *rope_p)
            qs, cs, krs, zs, _, _ = _mla_proj(xs, g_in, ws, *norms, *rope_s)
            op = _mla_attn(qp, kcat, v128)
            wkv = mla_w_kvb[j].reshape(KV_LORA, N_HEADS, NOPE + V_DIM)
            w_uk = jnp.pad(wkv[..., :NOPE].transpose(1, 2, 0), ((0, 0), (0, LANE - NOPE), (0, 0)))
            w_uv = jnp.pad(wkv[..., NOPE:].transpose(1, 0, 2), ((0, 0), (0, 0), (0, LANE - V_DIM)))
            q_lat = _bmm(qs[0], w_uk).transpose(1, 0, 2)
            q_rope = qs[0][:, :, NOPE:NOPE + ROPE].astype(F32).transpose(1, 0, 2)
            o_lat = _mla_sample(page_table, q_lat, q_rope, srow(cs[0]), srow(krs[0][:, :ROPE]),
                                pool(cache_mla_ckv), pool(cache_mla_krope), j)
            os_ = _bmm(o_lat.transpose(1, 0, 2), w_uv)[..., :V_DIM].transpose(1, 0, 2).reshape(ndb, ATT_WIDTH)
            w_out = mla_w_out[j]
            rows = {"mla_ckv_prompt": cp, "mla_krope_prompt": krp[..., :ROPE],
                    "mla_ckv_sample": cs.reshape(ndb, 1, KV_LORA), "mla_krope_sample": krs[0][:, :ROPE].reshape(ndb, 1, ROPE)}
        for name, r in rows.items():
            new.setdefault(name, []).append(r)
        ple = (ple_proj[i], ple_gate[i], ple_norm[i])
        xp = _out_ple(op.reshape(b * s, ATT_WIDTH), zp.reshape(b * s, ATT_WIDTH), xp.reshape(b * s, D_MODEL),
                      p_prompt[i].reshape(b * s, D_PLE), w_out, *ple).reshape(b, s, D_MODEL)
        xs = _out_ple(os_, zs.reshape(ndb, ATT_WIDTH), xs.reshape(ndb, D_MODEL),
                      p_sample[i].reshape(ndb, D_PLE), w_out, *ple).reshape(1, ndb, D_MODEL)
    stack = lambda name: jnp.stack(new[name])
    names = ("moba_k_prompt", "moba_v_prompt", "moba_k_sample", "moba_v_sample",
             "nsa_cmp_k_prompt", "nsa_cmp_v_prompt", "nsa_slc_k_prompt", "nsa_slc_v_prompt",
             "nsa_win_k_prompt", "nsa_win_v_prompt",
             "nsa_cmp_k_sample", "nsa_cmp_v_sample", "nsa_slc_k_sample", "nsa_slc_v_sample",
             "nsa_win_k_sample", "nsa_win_v_sample",
             "mla_ckv_prompt", "mla_krope_prompt", "mla_ckv_sample", "mla_krope_sample")
    return (xp, xs.reshape(ndb, 1, D_MODEL)) + tuple(stack(n) for n in names)
```
